```python
import jax
import jax.numpy as jnp
from jax import lax
import numpy as np

D_MODEL = 1024
BATCH = 1
SEQ = 16384
DEPTH = 1
DEC_BATCH = 8
DEC_SEQ = 2048
PAST_LEN = 128

POOL_WINDOWS = (2, 4, 8, 16)
N_POOL_GROUPS = len(POOL_WINDOWS)
POOL_WIDTH = D_MODEL
POOL_GROUP = POOL_WIDTH // N_POOL_GROUPS
CONV_WIDTH = D_MODEL
CONV_K = 3
N_EXPERTS = 16
EC_CAPACITY_FACTOR = 2
D_EXPERT = D_MODEL
ALPHA = (2.0 * DEPTH) ** 0.25
BETA = (8.0 * DEPTH) ** -0.25
LN_EPS = 1e-5
IN_COLS = POOL_WIDTH + 3 * CONV_WIDTH + 2 * D_MODEL
IN_SPLITS = (POOL_WIDTH,
             POOL_WIDTH + CONV_WIDTH,
             POOL_WIDTH + 2 * CONV_WIDTH,
             POOL_WIDTH + 3 * CONV_WIDTH,
             POOL_WIDTH + 3 * CONV_WIDTH + D_MODEL)

kernel_name = 'hybrid_pool_conv_ec_encoder'


def layer_norm(x, g, b):
    xf = x.astype(jnp.float32)
    mu = jnp.mean(xf, axis=-1, keepdims=True)
    var = jnp.mean(jnp.square(xf - mu), axis=-1, keepdims=True)
    return ((xf - mu) * lax.rsqrt(var + LN_EPS) * g.astype(jnp.float32) + b.astype(jnp.float32)).astype(x.dtype)


def pool_mixer(u, pool_w, pool_scale):
    b, s, _ = u.shape
    uf = u.astype(jnp.float32)
    csum = jnp.pad(jnp.cumsum(uf, axis=1), ((0, 0), (1, 0), (0, 0)))
    t = jnp.arange(s)
    groups = []
    for g, w in enumerate(POOL_WINDOWS):
        lo = jnp.clip(t - w // 2, 0, s)
        hi = jnp.clip(t + w - w // 2, 0, s)
        cnt = (hi - lo).astype(jnp.float32)[None, :, None]
        sl = slice(g * POOL_GROUP, (g + 1) * POOL_GROUP)
        c = csum[..., sl]
        mean = (jnp.take(c, hi, axis=1) - jnp.take(c, lo, axis=1)) / cnt
        groups.append(mean - uf[..., sl])
    pooled = jnp.stack(groups, axis=2).astype(u.dtype)
    mixed = jnp.einsum('bsgc,gcd->bsgd', pooled, pool_w)
    return mixed.reshape(b, s, POOL_WIDTH) * pool_scale


def short_conv_mixer(h, gate_b, gate_c, conv_w):
    s = h.shape[1]
    z = gate_c * h
    zp = jnp.pad(z, ((0, 0), (1, 1), (0, 0)))
    conv = zp[:, 0:s] * conv_w[0] + zp[:, 1:s + 1] * conv_w[1] + zp[:, 2:s + 2] * conv_w[2]
    return gate_b * conv


def mixer_sublayer(x, w_in, pool_w, pool_scale, w_pool_proj, conv_w, w_conv_out, w_o):
    proj = jnp.einsum('bsd,de->bse', x, w_in)
    u, h, gate_b, gate_c, g_pool, g_conv = jnp.split(proj, IN_SPLITS, axis=-1)
    y_pool = jnp.einsum('bsp,pd->bsd', pool_mixer(u, pool_w, pool_scale), w_pool_proj)
    y_conv = jnp.einsum('bsc,cd->bsd', short_conv_mixer(h, gate_b, gate_c, conv_w), w_conv_out)
    merged = jax.nn.sigmoid(g_pool) * y_pool + jax.nn.sigmoid(g_conv) * y_conv
    return jnp.einsum('bsd,de->bse', merged, w_o)


def expert_choice_moe(x, w_router, w_gate, w_up, w_down):
    b, s, d = x.shape
    n_tok = b * s
    cap = EC_CAPACITY_FACTOR * n_tok // N_EXPERTS
    xt = x.reshape(n_tok, d)
    logits = jnp.einsum('td,de->te', xt.astype(jnp.float32), w_router.astype(jnp.float32))
    affinity = jax.nn.softmax(logits, axis=-1)
    gates, idx = lax.top_k(affinity.T, cap)
    xe = jnp.take(xt, idx, axis=0)
    hg = jnp.einsum('ecd,edf->ecf', xe, w_gate)
    hu = jnp.einsum('ecd,edf->ecf', xe, w_up)
    ye = jnp.einsum('ecf,efd->ecd', jax.nn.silu(hg) * hu, w_down)
    ye = ye.astype(jnp.float32) * gates[..., None]
    out = jnp.zeros((n_tok, d), jnp.float32).at[idx.reshape(-1)].add(ye.reshape(-1, d))
    return out.astype(x.dtype).reshape(b, s, d)


def encoder_trunk(x, ln_in_g, ln_in_b, w_in, pool_w, pool_scale, w_pool_proj, conv_w, w_conv_out,
                  w_o, ln1_g, ln1_b, w_router, w_gate, w_up, w_down, ln2_g, ln2_b):
    x = layer_norm(x, ln_in_g, ln_in_b)
    for l in range(DEPTH):
        m = mixer_sublayer(x, w_in[l], pool_w[l], pool_scale[l], w_pool_proj[l], conv_w[l],
                           w_conv_out[l], w_o[l])
        x = layer_norm(ALPHA * x + m, ln1_g[l], ln1_b[l])
        f = expert_choice_moe(x, w_router[l], w_gate[l], w_up[l], w_down[l])
        x = layer_norm(ALPHA * x + f, ln2_g[l], ln2_b[l])
    return x


def _normal(key, shape, scale):
    return jax.random.normal(key, shape, jnp.float32) * scale


def setup_inputs(seed: int = 0) -> dict:
    key = jax.random.key(seed)
    ks = jax.random.split(key, 20)
    L, D = DEPTH, D_MODEL
    return {
        'x_prompt': _normal(ks[0], (BATCH, SEQ, D), 1.0),
        'x_sample': _normal(ks[1], (DEC_BATCH, DEC_SEQ, D), 1.0),
        'ln_in_g': 1.0 + _normal(ks[2], (D,), 0.02),
        'ln_in_b': _normal(ks[3], (D,), 0.02),
        'w_in': _normal(ks[4], (L, D, IN_COLS), D ** -0.5),
        'pool_w': _normal(ks[5], (L, N_POOL_GROUPS, POOL_GROUP, POOL_GROUP), POOL_GROUP ** -0.5),
        'pool_scale': 1.0 + _normal(ks[6], (L, POOL_WIDTH), 0.02),
        'w_pool_proj': _normal(ks[7], (L, POOL_WIDTH, D), BETA * POOL_WIDTH ** -0.5),
        'conv_w': _normal(ks[8], (L, CONV_K, CONV_WIDTH), CONV_K ** -0.5),
        'w_conv_out': _normal(ks[9], (L, CONV_WIDTH, D), BETA * CONV_WIDTH ** -0.5),
        'w_o': _normal(ks[10], (L, D, D), BETA * D ** -0.5),
        'ln1_g': 1.0 + _normal(ks[11], (L, D), 0.02),
        'ln1_b': _normal(ks[12], (L, D), 0.02),
        'w_router': _normal(ks[13], (L, D, N_EXPERTS), D ** -0.5),
        'w_gate': _normal(ks[14], (L, N_EXPERTS, D, D_EXPERT), D ** -0.5),
        'w_up': _normal(ks[15], (L, N_EXPERTS, D, D_EXPERT), D ** -0.5),
        'w_down': _normal(ks[16], (L, N_EXPERTS, D_EXPERT, D), BETA * D_EXPERT ** -0.5),
        'ln2_g': 1.0 + _normal(ks[17], (L, D), 0.02),
        'ln2_b': _normal(ks[18], (L, D), 0.02),
    }


def reference(x_prompt, x_sample, ln_in_g, ln_in_b, w_in, pool_w, pool_scale, w_pool_proj, conv_w,
              w_conv_out, w_o, ln1_g, ln1_b, w_router, w_gate, w_up, w_down, ln2_g, ln2_b):
    y_prompt = encoder_trunk(x_prompt, ln_in_g, ln_in_b, w_in, pool_w, pool_scale, w_pool_proj, conv_w,
                             w_conv_out, w_o, ln1_g, ln1_b, w_router, w_gate, w_up, w_down, ln2_g, ln2_b)
    y_sample = encoder_trunk(x_sample, ln_in_g, ln_in_b, w_in, pool_w, pool_scale, w_pool_proj, conv_w,
                             w_conv_out, w_o, ln1_g, ln1_b, w_router, w_gate, w_up, w_down, ln2_g, ln2_b)
    return (y_prompt, y_sample)
```

```python
import functools

import jax
import jax.numpy as jnp
from jax import lax
from jax.experimental import pallas as pl
from jax.experimental.pallas import tpu as pltpu

F32 = jnp.float32
BF16 = jnp.bfloat16

D_MODEL = 1024
N_EXPERTS = 16
EC_CAPACITY_FACTOR = 2
POOL_WINDOWS = (2, 4, 8, 16)
POOL_GROUP = D_MODEL // len(POOL_WINDOWS)
ALPHA = 2.0 ** 0.25
LN_EPS = 1e-5
IN_COLS = 6 * D_MODEL

LANES = 128
SUBLANES = 8
BF16_ROWS = 16
TM = 512
HALO = SUBLANES
TB = 512
CBW = 128
FFN_ROWS = 512
GATE_COLS = LANES
VMEM_LIMIT = 56 * 1024 * 1024


def _layer_norm(x, g, b):
    mu = jnp.mean(x, axis=-1, keepdims=True)
    xc = x - mu
    var = jnp.mean(xc * xc, axis=-1, keepdims=True)
    return xc * lax.rsqrt(var + LN_EPS) * g + b


def _dot(a, b):
    return jnp.dot(a, b, preferred_element_type=F32)


def _dot_nt(a, b):
    return lax.dot_general(a, b, (((1,), (1,)), ((), ())), preferred_element_type=F32)


def _mixer_kernel(seq_len, x_ref, xprev_ref, xnext_ref, lng_ref, lnb_ref, win_ref, poolw_ref,
                  pscale_ref, wpp_ref, convw_ref, wco_ref, wo_ref, ln1g_ref, ln1b_ref,
                  wrcat_ref, wrhi_ref, x1_ref, xext_ref, afft_ref, u_ext, z_ext):
    i = pl.program_id(0)
    p0 = lax.rem(i * TM, seq_len)
    rows = TM + 2 * HALO
    main = slice(HALO, HALO + TM)

    xa = jnp.concatenate([xprev_ref[...], x_ref[...], xnext_ref[...]], axis=0)
    xna = _layer_norm(xa, lng_ref[...], lnb_ref[...])
    xb = xna.astype(BF16)
    xn = xna[main]
    xb_main = xn.astype(BF16)

    ridx = lax.broadcasted_iota(jnp.int32, (rows, 1), 0)
    lo_keep = jnp.where(p0 == 0, HALO, 0)
    hi_keep = jnp.where(p0 + TM == seq_len, HALO + TM, rows)
    keep = jnp.logical_and(ridx >= lo_keep, ridx < hi_keep)

    def proj(lhs, k):
        return _dot(lhs, win_ref[:, k * D_MODEL:(k + 1) * D_MODEL])

    u_ext[...] = jnp.where(keep, proj(xb, 0), 0.0)
    z_ext[...] = jnp.where(keep, proj(xb, 3) * proj(xb, 1), 0.0)

    pos = p0 + lax.broadcasted_iota(jnp.int32, (TM, 1), 0)
    mixed = []
    for gi, w in enumerate(POOL_WINDOWS):
        cols = slice(gi * POOL_GROUP, (gi + 1) * POOL_GROUP)
        s = None
        for j in range(-(w // 2), w - w // 2):
            v = u_ext[HALO + j:HALO + j + TM, cols]
            s = v if s is None else s + v
        cnt = (jnp.minimum(pos + (w - w // 2), seq_len) - jnp.maximum(pos - w // 2, 0)).astype(F32)
        pooled = s / cnt - u_ext[main, cols]
        mixed.append(_dot(pooled.astype(BF16), poolw_ref[gi]))
    mixed = jnp.concatenate(mixed, axis=1) * pscale_ref[...]
    y_pool = _dot(mixed.astype(BF16), wpp_ref[...])

    cw = convw_ref[...]
    conv = (z_ext[HALO - 1:HALO - 1 + TM, :] * cw[0:1, :] + z_ext[main, :] * cw[1:2, :]
            + z_ext[HALO + 1:HALO + 1 + TM, :] * cw[2:3, :])
    y_conv = _dot((proj(xb_main, 2) * conv).astype(BF16), wco_ref[...])

    merged = jax.nn.sigmoid(proj(xb_main, 4)) * y_pool + jax.nn.sigmoid(proj(xb_main, 5)) * y_conv
    m = _dot(merged.astype(BF16), wo_ref[...])
    x1 = _layer_norm(ALPHA * xn + m, ln1g_ref[...], ln1b_ref[...])
    x1_ref[...] = x1

    x1h = x1.astype(BF16)
    x1l = (x1 - x1h.astype(F32)).astype(BF16)
    lg = _dot(x1h, wrcat_ref[...])
    logits = lg[:, :LANES] + lg[:, LANES:] + _dot(x1l, wrhi_ref[...])
    lane = lax.broadcasted_iota(jnp.int32, (TM, LANES), 1)
    logits = jnp.where(lane < N_EXPERTS, logits, -1e30)
    ex = jnp.exp(logits - jnp.max(logits, axis=1, keepdims=True))
    aff = ex / jnp.sum(ex, axis=1, keepdims=True)
    afft_ref[...] = aff.T[:N_EXPERTS, :]

    a1 = aff.astype(BF16).astype(F32)
    r1 = aff - a1
    a2 = r1.astype(BF16).astype(F32)
    a3 = (r1 - a2).astype(BF16).astype(F32)
    ext = a1 + pltpu.roll(a2, N_EXPERTS, 1) + pltpu.roll(a3, 2 * N_EXPERTS, 1)
    xext_ref[:, :D_MODEL] = x1h
    xext_ref[:, D_MODEL:] = ext.astype(BF16)


def _mixer(x, seq_len, p):
    n_tok = x.shape[0]
    n_tiles = n_tok // TM
    assert n_tok % TM == 0 and seq_len % TM == 0
    tiles_per_halo = TM // HALO
    last_halo = n_tok // HALO - 1

    def const(shape):
        return pl.BlockSpec(shape, lambda i: (0,) * len(shape), pipeline_mode=pl.Buffered(1))

    in_specs = [
        pl.BlockSpec((TM, D_MODEL), lambda i: (i, 0)),
        pl.BlockSpec((HALO, D_MODEL), lambda i: (jnp.maximum(i * tiles_per_halo - 1, 0), 0)),
        pl.BlockSpec((HALO, D_MODEL), lambda i: (jnp.minimum((i + 1) * tiles_per_halo, last_halo), 0)),
        const((1, D_MODEL)), const((1, D_MODEL)),
        const((D_MODEL, IN_COLS)),
        const((len(POOL_WINDOWS), POOL_GROUP, POOL_GROUP)),
        const((1, D_MODEL)),
        const((D_MODEL, D_MODEL)),
        const((3, D_MODEL)),
        const((D_MODEL, D_MODEL)),
        const((D_MODEL, D_MODEL)),
        const((1, D_MODEL)), const((1, D_MODEL)),
        const((D_MODEL, 2 * LANES)), const((D_MODEL, LANES)),
    ]
    out_specs = [
        pl.BlockSpec((TM, D_MODEL), lambda i: (i, 0)),
        pl.BlockSpec((TM, D_MODEL + GATE_COLS), lambda i: (i, 0)),
        pl.BlockSpec((N_EXPERTS, TM), lambda i: (0, i)),
    ]
    out_shape = [
        jax.ShapeDtypeStruct((n_tok, D_MODEL), F32),
        jax.ShapeDtypeStruct((n_tok, D_MODEL + GATE_COLS), BF16),
        jax.ShapeDtypeStruct((N_EXPERTS, n_tok), F32),
    ]
    return pl.pallas_call(
        functools.partial(_mixer_kernel, seq_len),
        grid=(n_tiles,),
        in_specs=in_specs,
        out_specs=out_specs,
        out_shape=out_shape,
        scratch_shapes=[pltpu.VMEM((TM + 2 * HALO, D_MODEL), F32),
                        pltpu.VMEM((TM + 2 * HALO, D_MODEL), F32)],
        compiler_params=pltpu.CompilerParams(dimension_semantics=("arbitrary",),
                                             vmem_limit_bytes=VMEM_LIMIT),
        name="mixer",
    )(x, x, x, p["ln_in_g"], p["ln_in_b"], p["w_in"], p["pool_w"], p["pool_scale"], p["w_pool_proj"],
      p["conv_w"], p["w_conv_out"], p["w_o"], p["ln1_g"], p["ln1_b"], p["wr_cat"], p["wr_hi"])


def _select_kernel(cap, n_tiles, afft_ref, sel_ref, starts_ref, eq_ref):
    n_tok = afft_ref.shape[1]
    bits = pltpu.bitcast(afft_ref[...], jnp.int32)

    def count(mask):
        return jnp.sum(jnp.where(mask, 1.0, 0.0), axis=1, keepdims=True)

    def bit_step(it, thr):
        cand = jnp.bitwise_or(thr, jnp.left_shift(jnp.int32(1), 30 - it))
        return jnp.where(count(bits >= cand) >= cap, cand, thr)

    thr = lax.fori_loop(0, 31, bit_step, jnp.zeros((N_EXPERTS, 1), jnp.int32))
    gt = bits > thr
    eq = bits == thr
    need = cap - count(gt)
    excess = jnp.max(count(eq) - need)
    sel_ref[...] = jnp.where(gt, 1.0, 0.0)
    eq_ref[...] = jnp.where(eq, 1.0, 0.0)

    @pl.when(excess <= 0.0)
    def _():
        sel_ref[...] = sel_ref[...] + eq_ref[...]

    @pl.when(excess > 0.0)
    def _():
        r = lax.broadcasted_iota(jnp.int32, (LANES, LANES), 0)
        c = lax.broadcasted_iota(jnp.int32, (LANES, LANES), 1)
        tri = jnp.where(r <= c, 1.0, 0.0).astype(BF16)

        def chunk(ci, seen):
            off = pl.multiple_of(ci * LANES, LANES)
            e_blk = eq_ref[:, pl.ds(off, LANES)]
            incl = _dot(e_blk.astype(BF16), tri)
            rank = seen + incl - e_blk
            sel_ref[:, pl.ds(off, LANES)] = sel_ref[:, pl.ds(off, LANES)] + jnp.where(rank < need, e_blk, 0.0)
            return seen + incl[:, LANES - 1:LANES]

        lax.fori_loop(0, n_tok // LANES, chunk, jnp.zeros((N_EXPERTS, 1), F32))

    lane = lax.broadcasted_iota(jnp.int32, (N_EXPERTS, LANES), 1)
    run = jnp.zeros((N_EXPERTS, 1), F32)
    starts = jnp.zeros((N_EXPERTS, LANES), F32)
    for j in range(n_tiles):
        starts = jnp.where(lane == j, run, starts)
        run = run + jnp.sum(sel_ref[:, j * TB:(j + 1) * TB], axis=1, keepdims=True)
    starts_ref[...] = starts.astype(jnp.int32)


def _select(afft, cap):
    n_tok = afft.shape[1]
    n_tiles = n_tok // TB
    assert n_tiles <= LANES
    return pl.pallas_call(
        functools.partial(_select_kernel, cap, n_tiles),
        out_shape=[jax.ShapeDtypeStruct((N_EXPERTS, n_tok), F32),
                   jax.ShapeDtypeStruct((N_EXPERTS, LANES), jnp.int32)],
        scratch_shapes=[pltpu.VMEM((N_EXPERTS, n_tok), F32)],
        compiler_params=pltpu.CompilerParams(vmem_limit_bytes=VMEM_LIMIT),
        name="select",
    )(afft)


def _tile_scalars(starts_ref, j):
    out = []
    for e in range(N_EXPERTS):
        st = starts_ref[j, e]
        a = jnp.bitwise_and(st, -BF16_ROWS)
        out.append((a, st - a, starts_ref[j + 1, e] - st))
    return out


def _n_passes(scal):
    last = scal[0][1] + scal[0][2]
    for _, off, n in scal[1:]:
        last = jnp.maximum(last, off + n)
    return lax.div(last + (CBW - 1), CBW)


def _dispatch_kernel(starts_ref, sel_ref, xext_ref, xe_hbm, gates_hbm, wx, wg, carry_x, carry_g, sem):
    j = pl.program_id(0)

    @pl.when(j == 0)
    def _():
        carry_x[...] = jnp.zeros_like(carry_x)
        carry_g[...] = jnp.zeros_like(carry_g)
        wx[...] = jnp.zeros_like(wx)
        wg[...] = jnp.zeros_like(wg)

    scal = _tile_scalars(starts_ref, j)
    sel = sel_ref[...]
    r = lax.broadcasted_iota(jnp.int32, (TB, TB), 0)
    c = lax.broadcasted_iota(jnp.int32, (TB, TB), 1)
    rank = _dot(sel.astype(BF16), jnp.where(r < c, 1.0, 0.0).astype(BF16))
    sub = lax.broadcasted_iota(jnp.int32, (N_EXPERTS, 1), 0)
    off_col = jnp.zeros((N_EXPERTS, 1), F32)
    for e, (_, off, _) in enumerate(scal):
        off_col = jnp.where(sub == e, off.astype(F32), off_col)
    pos = jnp.where(sel > 0.5, rank + off_col, -1.0)
    row = lax.broadcasted_iota(jnp.int32, (CBW, 1), 0).astype(F32)

    def copies(e, a, k):
        dst = pl.ds(pl.multiple_of(a + k * CBW, BF16_ROWS), CBW)
        return (pltpu.make_async_copy(wx.at[e, pl.ds(0, CBW)], xe_hbm.at[e, dst], sem.at[0]),
                pltpu.make_async_copy(wg.at[e, pl.ds(0, CBW)], gates_hbm.at[e, dst], sem.at[1]))

    def window_pass(k, first):
        krow = row if first else row + (k * CBW).astype(F32)
        onehot = [jnp.where(pos[e:e + 1, :] == krow, 1.0, 0.0).astype(BF16) for e in range(N_EXPERTS)]
        w = _dot(jnp.concatenate(onehot, axis=0), xext_ref[...])

        def guarded(e, fn):
            if first:
                fn()
            else:
                _, off, n = scal[e]
                pl.when(off + n > k * CBW)(fn)

        for e, (a, off, n) in enumerate(scal):
            def emit(e=e, a=a, off=off, n=n):
                we = w[e * CBW:(e + 1) * CBW]
                wx[e, 0:CBW, :] = we[:, :D_MODEL].astype(BF16)
                wg[e, 0:CBW, :] = we[:, D_MODEL:]
                if first:
                    wx[e, 0:BF16_ROWS, :] = (wx[e, 0:BF16_ROWS, :].astype(F32)
                                             + carry_x[e].astype(F32)).astype(BF16)
                    wg[e, 0:BF16_ROWS, :] = wg[e, 0:BF16_ROWS, :] + carry_g[e]
                for cp in copies(e, a, k):
                    cp.start()

                @pl.when(off + n <= (k + 1) * CBW)
                def _():
                    d = pl.multiple_of(jnp.bitwise_and(a + off + n, -BF16_ROWS) - a - k * CBW, BF16_ROWS)
                    carry_x[e] = wx[e, pl.ds(d, BF16_ROWS), :]
                    carry_g[e] = wg[e, pl.ds(d, BF16_ROWS), :]
            guarded(e, emit)

        for e, (a, _, _) in enumerate(scal):
            def drain(e=e, a=a):
                for cp in copies(e, a, k):
                    cp.wait()
            guarded(e, drain)

    window_pass(jnp.int32(0), True)

    def more(k, carry):
        window_pass(k, False)
        return carry

    lax.fori_loop(1, _n_passes(scal), more, 0)

    @pl.when(j == pl.num_programs(0) - 1)
    def _():
        wx[0, 0:CBW, :] = jnp.zeros((CBW, D_MODEL), BF16)
        wg[0, 0:CBW, :] = jnp.zeros((CBW, GATE_COLS), F32)
        tail = pl.ds(xe_hbm.shape[1] - CBW, CBW)
        pads = [(pltpu.make_async_copy(wx.at[0, pl.ds(0, CBW)], xe_hbm.at[e, tail], sem.at[0]),
                 pltpu.make_async_copy(wg.at[0, pl.ds(0, CBW)], gates_hbm.at[e, tail], sem.at[1]))
                for e in range(N_EXPERTS)]
        for cx, cg in pads:
            cx.start()
            cg.start()
        for cx, cg in pads:
            cx.wait()
            cg.wait()


def _dispatch(starts_tab, sel, xext, cap):
    n_tok = sel.shape[1]
    n_tiles = n_tok // TB
    rows = cap + CBW
    grid_spec = pltpu.PrefetchScalarGridSpec(
        num_scalar_prefetch=1,
        grid=(n_tiles,),
        in_specs=[pl.BlockSpec((N_EXPERTS, TB), lambda j, s: (0, j)),
                  pl.BlockSpec((TB, D_MODEL + GATE_COLS), lambda j, s: (j, 0))],
        out_specs=[pl.BlockSpec(memory_space=pl.ANY), pl.BlockSpec(memory_space=pl.ANY)],
        scratch_shapes=[pltpu.VMEM((N_EXPERTS, CBW + BF16_ROWS, D_MODEL), BF16),
                        pltpu.VMEM((N_EXPERTS, CBW + BF16_ROWS, GATE_COLS), F32),
                        pltpu.VMEM((N_EXPERTS, BF16_ROWS, D_MODEL), BF16),
                        pltpu.VMEM((N_EXPERTS, BF16_ROWS, GATE_COLS), F32),
                        pltpu.SemaphoreType.DMA((2,))],
    )
    return pl.pallas_call(
        _dispatch_kernel,
        grid_spec=grid_spec,
        out_shape=[jax.ShapeDtypeStruct((N_EXPERTS, rows, D_MODEL), BF16),
                   jax.ShapeDtypeStruct((N_EXPERTS, rows, GATE_COLS), F32)],
        compiler_params=pltpu.CompilerParams(dimension_semantics=("arbitrary",),
                                             vmem_limit_bytes=VMEM_LIMIT),
        name="dispatch",
    )(starts_tab, sel, xext)


def _ffn_kernel(cap, xe_ref, gates_ref, wg_ref, wu_ref, wd_ref, ye_ref, wg_b, wu_b, wd_b):
    e = pl.program_id(0)
    wg_b[...] = wg_ref[0].astype(BF16)
    wu_b[...] = wu_ref[0].astype(BF16)
    wd_b[...] = wd_ref[0].astype(BF16)
    lane = lax.broadcasted_iota(jnp.int32, (FFN_ROWS, GATE_COLS), 1)
    mine = jnp.logical_and(jnp.bitwise_and(lane, N_EXPERTS - 1) == e, lane < 3 * N_EXPERTS)
    for c0 in range(0, cap, FFN_ROWS):
        rows = slice(c0, c0 + FFN_ROWS)
        x = xe_ref[0, rows, :]
        act = jax.nn.silu(_dot(x, wg_b[...])) * _dot(x, wu_b[...])
        y = _dot(act.astype(BF16), wd_b[...])
        gate = jnp.sum(jnp.where(mine, gates_ref[0, rows, :], 0.0), axis=1, keepdims=True)
        ye_ref[0, rows, :] = (y * gate).astype(BF16)
    ye_ref[0, cap:, :] = jnp.zeros((ye_ref.shape[1] - cap, D_MODEL), BF16)


def _ffn(xe, gates, w_gate, w_up, w_down, cap):
    rows = xe.shape[1]
    assert cap % FFN_ROWS == 0
    wspec = pl.BlockSpec((1, D_MODEL, D_MODEL), lambda e: (e, 0, 0))
    return pl.pallas_call(
        functools.partial(_ffn_kernel, cap),
        grid=(N_EXPERTS,),
        in_specs=[pl.BlockSpec((1, rows, D_MODEL), lambda e: (e, 0, 0)),
                  pl.BlockSpec((1, rows, GATE_COLS), lambda e: (e, 0, 0)),
                  wspec, wspec, wspec],
        out_specs=pl.BlockSpec((1, rows, D_MODEL), lambda e: (e, 0, 0)),
        out_shape=jax.ShapeDtypeStruct((N_EXPERTS, rows, D_MODEL), BF16),
        scratch_shapes=[pltpu.VMEM((D_MODEL, D_MODEL), BF16)] * 3,
        compiler_params=pltpu.CompilerParams(dimension_semantics=("arbitrary",),
                                             vmem_limit_bytes=VMEM_LIMIT),
        name="ffn",
    )(xe, gates, w_gate, w_up, w_down)


def _combine_kernel(starts_ref, sel_ref, x1_ref, ln2g_ref, ln2b_ref, ye_hbm, out_ref, ybuf, sem):
    j = pl.program_id(0)
    scal = _tile_scalars(starts_ref, j)

    def copy(e, a, k):
        src = pl.ds(pl.multiple_of(a + k * CBW, BF16_ROWS), CBW)
        return pltpu.make_async_copy(ye_hbm.at[e, src], ybuf.at[e], sem.at[0])

    for e, (a, _, _) in enumerate(scal):
        copy(e, a, 0).start()

    selp = jnp.concatenate([sel_ref[...], jnp.zeros((LANES - N_EXPERTS, TB), F32)], axis=0).astype(BF16)
    r = lax.broadcasted_iota(jnp.int32, (TB, TB), 0)
    c = lax.broadcasted_iota(jnp.int32, (TB, TB), 1)
    rank_t = _dot_nt(jnp.where(r > c, 1.0, 0.0).astype(BF16), selp)
    sel_t = _dot_nt(jnp.where(r == c, 1.0, 0.0).astype(BF16), selp)
    lane = lax.broadcasted_iota(jnp.int32, (1, LANES), 1)
    off_row = jnp.zeros((1, LANES), F32)
    for e, (_, off, _) in enumerate(scal):
        off_row = jnp.where(lane == e, off.astype(F32), off_row)
    pos = jnp.where(sel_t > 0.5, rank_t + off_row, -1.0)
    col = lax.broadcasted_iota(jnp.int32, (1, CBW), 1).astype(F32)

    def onehot(k_rows):
        return jnp.concatenate([jnp.where(pos[:, e:e + 1] == col + k_rows, 1.0, 0.0).astype(BF16)
                                for e in range(N_EXPERTS)], axis=1)

    p0 = onehot(0.0)
    for e, (a, _, _) in enumerate(scal):
        copy(e, a, 0).wait()
    out_ref[...] = _dot(p0, ybuf[...].reshape(N_EXPERTS * CBW, D_MODEL))

    def more(k, carry):
        for e, (a, off, n) in enumerate(scal):
            @pl.when(off + n > k * CBW)
            def _(e=e, a=a):
                cp = copy(e, a, k)
                cp.start()
                cp.wait()
        out_ref[...] += _dot(onehot((k * CBW).astype(F32)), ybuf[...].reshape(N_EXPERTS * CBW, D_MODEL))
        return carry

    lax.fori_loop(1, _n_passes(scal), more, 0)
    out_ref[...] = _layer_norm(ALPHA * x1_ref[...] + out_ref[...], ln2g_ref[...], ln2b_ref[...])


def _combine(starts_tab, sel, x1, ye, ln2_g, ln2_b):
    n_tok = sel.shape[1]
    n_tiles = n_tok // TB
    grid_spec = pltpu.PrefetchScalarGridSpec(
        num_scalar_prefetch=1,
        grid=(n_tiles,),
        in_specs=[pl.BlockSpec((N_EXPERTS, TB), lambda j, s: (0, j)),
                  pl.BlockSpec((TB, D_MODEL), lambda j, s: (j, 0)),
                  pl.BlockSpec((1, D_MODEL), lambda j, s: (0, 0)),
                  pl.BlockSpec((1, D_MODEL), lambda j, s: (0, 0)),
                  pl.BlockSpec(memory_space=pl.ANY)],
        out_specs=pl.BlockSpec((TB, D_MODEL), lambda j, s: (j, 0)),
        scratch_shapes=[pltpu.VMEM((N_EXPERTS, CBW, D_MODEL), BF16),
                        pltpu.SemaphoreType.DMA((1,))],
    )
    return pl.pallas_call(
        _combine_kernel,
        grid_spec=grid_spec,
        out_shape=jax.ShapeDtypeStruct((n_tok, D_MODEL), F32),
        compiler_params=pltpu.CompilerParams(dimension_semantics=("arbitrary",),
                                             vmem_limit_bytes=VMEM_LIMIT),
        name="combine",
    )(starts_tab, sel, x1, ln2_g, ln2_b, ye)


def _trunk(x, p):
    b, s, d = x.shape
    n_tok = b * s
    cap = EC_CAPACITY_FACTOR * n_tok // N_EXPERTS
    n_tiles = n_tok // TB
    x1, xext, afft = _mixer(x.reshape(n_tok, d), s, p)
    sel, starts = _select(afft, cap)
    starts_tab = jnp.concatenate([starts[:, :n_tiles].T, jnp.full((1, N_EXPERTS), cap, jnp.int32)], axis=0)
    xe, gates = _dispatch(starts_tab, sel, xext, cap)
    ye = _ffn(xe, gates, p["w_gate"], p["w_up"], p["w_down"], cap)
    y = _combine(starts_tab, sel, x1, ye, p["ln2_g"], p["ln2_b"])
    return y.reshape(b, s, d)


def kernel(x_prompt, x_sample, ln_in_g, ln_in_b, w_in, pool_w, pool_scale, w_pool_proj, conv_w, w_conv_out, w_o, ln1_g, ln1_b, w_router, w_gate, w_up, w_down, ln2_g, ln2_b):
    assert w_in.shape == (1, D_MODEL, IN_COLS)
    wr = jnp.pad(w_router[0], ((0, 0), (0, LANES - N_EXPERTS)))
    wr_hi = wr.astype(BF16)
    wr_lo = (wr - wr_hi.astype(F32)).astype(BF16)
    p = {
        "ln_in_g": ln_in_g.reshape(1, D_MODEL), "ln_in_b": ln_in_b.reshape(1, D_MODEL),
        "w_in": w_in[0].astype(BF16), "pool_w": pool_w[0].astype(BF16), "pool_scale": pool_scale,
        "w_pool_proj": w_pool_proj[0].astype(BF16), "conv_w": conv_w[0],
        "w_conv_out": w_conv_out[0].astype(BF16), "w_o": w_o[0].astype(BF16),
        "ln1_g": ln1_g, "ln1_b": ln1_b,
        "wr_cat": jnp.concatenate([wr_hi, wr_lo], axis=1), "wr_hi": wr_hi,
        "w_gate": w_gate[0], "w_up": w_up[0], "w_down": w_down[0],
        "ln2_g": ln2_g, "ln2_b": ln2_b,
    }
    return _trunk(x_prompt, p), _trunk(x_sample, p)
```

```python
import functools

import jax
import jax.numpy as jnp
from jax import lax
from jax.experimental import pallas as pl
from jax.experimental.pallas import tpu as pltpu

F32 = jnp.float32
BF16 = jnp.bfloat16

D_MODEL = 1024
N_EXPERTS = 16
EC_CAPACITY_FACTOR = 2
POOL_WINDOWS = (2, 4, 8, 16)
POOL_GROUP = D_MODEL // len(POOL_WINDOWS)
ALPHA = 2.0 ** 0.25
LN_EPS = 1e-5
IN_COLS = 6 * D_MODEL

LANES = 128
SUBLANES = 8
BF16_ROWS = 16
TM = 512
HALO = SUBLANES
TB = 512
CBW = 128
FFN_ROWS = 512
GATE_COLS = LANES
VMEM_LIMIT = 56 * 1024 * 1024


def _layer_norm(x, g, b):
    mu = jnp.mean(x, axis=-1, keepdims=True)
    xc = x - mu
    var = jnp.mean(xc * xc, axis=-1, keepdims=True)
    return xc * lax.rsqrt(var + LN_EPS) * g + b


def _dot(a, b):
    return jnp.dot(a, b, preferred_element_type=F32)


def _onehot(mask):
    return jnp.where(mask, 1.0, 0.0).astype(BF16)


def _mixer_kernel(seq_len, x_ref, xprev_ref, xnext_ref, lng_ref, lnb_ref, win_ref, poolw_ref,
                  pscale_ref, wpp_ref, convw_ref, wco_ref, wo_ref, ln1g_ref, ln1b_ref,
                  wrcat_ref, wrhi_ref, x1_ref, xext_ref, afft_ref, u_ext, z_ext):
    i = pl.program_id(0)
    p0 = lax.rem(i * TM, seq_len)
    rows = TM + 2 * HALO
    main = slice(HALO, HALO + TM)

    xa = jnp.concatenate([xprev_ref[...], x_ref[...], xnext_ref[...]], axis=0)
    xna = _layer_norm(xa, lng_ref[...], lnb_ref[...])
    xb = xna.astype(BF16)
    xn = xna[main]
    xb_main = xn.astype(BF16)

    ridx = lax.broadcasted_iota(jnp.int32, (rows, 1), 0)
    lo_keep = jnp.where(p0 == 0, HALO, 0)
    hi_keep = jnp.where(p0 + TM == seq_len, HALO + TM, rows)
    keep = jnp.logical_and(ridx >= lo_keep, ridx < hi_keep)

    def proj(lhs, k):
        return _dot(lhs, win_ref[:, k * D_MODEL:(k + 1) * D_MODEL])

    u_ext[...] = jnp.where(keep, proj(xb, 0), 0.0)
    z_ext[...] = jnp.where(keep, proj(xb, 3) * proj(xb, 1), 0.0)

    pos = p0 + lax.broadcasted_iota(jnp.int32, (TM, 1), 0)
    mixed = []
    for gi, w in enumerate(POOL_WINDOWS):
        cols = slice(gi * POOL_GROUP, (gi + 1) * POOL_GROUP)
        s = None
        for j in range(-(w // 2), w - w // 2):
            v = u_ext[HALO + j:HALO + j + TM, cols]
            s = v if s is None else s + v
        cnt = (jnp.minimum(pos + (w - w // 2), seq_len) - jnp.maximum(pos - w // 2, 0)).astype(F32)
        pooled = s / cnt - u_ext[main, cols]
        mixed.append(_dot(pooled.astype(BF16), poolw_ref[gi]))
    mixed = jnp.concatenate(mixed, axis=1) * pscale_ref[...]
    y_pool = _dot(mixed.astype(BF16), wpp_ref[...])

    cw = convw_ref[...]
    conv = (z_ext[HALO - 1:HALO - 1 + TM, :] * cw[0:1, :] + z_ext[main, :] * cw[1:2, :]
            + z_ext[HALO + 1:HALO + 1 + TM, :] * cw[2:3, :])
    y_conv = _dot((proj(xb_main, 2) * conv).astype(BF16), wco_ref[...])

    merged = jax.nn.sigmoid(proj(xb_main, 4)) * y_pool + jax.nn.sigmoid(proj(xb_main, 5)) * y_conv
    m = _dot(merged.astype(BF16), wo_ref[...])
    x1 = _layer_norm(ALPHA * xn + m, ln1g_ref[...], ln1b_ref[...])
    x1_ref[...] = x1

    x1h = x1.astype(BF16)
    x1l = (x1 - x1h.astype(F32)).astype(BF16)
    lg = _dot(x1h, wrcat_ref[...])
    logits = lg[:, :LANES] + lg[:, LANES:] + _dot(x1l, wrhi_ref[...])
    lane = lax.broadcasted_iota(jnp.int32, (TM, LANES), 1)
    logits = jnp.where(lane < N_EXPERTS, logits, -1e30)
    ex = jnp.exp(logits - jnp.max(logits, axis=1, keepdims=True))
    aff = ex / jnp.sum(ex, axis=1, keepdims=True)
    afft_ref[...] = aff.T[:N_EXPERTS, :]

    a1 = aff.astype(BF16).astype(F32)
    r1 = aff - a1
    a2 = r1.astype(BF16).astype(F32)
    a3 = (r1 - a2).astype(BF16).astype(F32)
    ext = a1 + pltpu.roll(a2, N_EXPERTS, 1) + pltpu.roll(a3, 2 * N_EXPERTS, 1)
    xext_ref[:, :D_MODEL] = x1h
    xext_ref[:, D_MODEL:] = ext.astype(BF16)


def _mixer(x, seq_len, p):
    n_tok = x.shape[0]
    n_tiles = n_tok // TM
    assert n_tok % TM == 0 and seq_len % TM == 0
    tiles_per_halo = TM // HALO
    last_halo = n_tok // HALO - 1

    def const(shape):
        return pl.BlockSpec(shape, lambda i: (0,) * len(shape), pipeline_mode=pl.Buffered(1))

    in_specs = [
        pl.BlockSpec((TM, D_MODEL), lambda i: (i, 0)),
        pl.BlockSpec((HALO, D_MODEL), lambda i: (jnp.maximum(i * tiles_per_halo - 1, 0), 0)),
        pl.BlockSpec((HALO, D_MODEL), lambda i: (jnp.minimum((i + 1) * tiles_per_halo, last_halo), 0)),
        const((1, D_MODEL)), const((1, D_MODEL)),
        const((D_MODEL, IN_COLS)),
        const((len(POOL_WINDOWS), POOL_GROUP, POOL_GROUP)),
        const((1, D_MODEL)),
        const((D_MODEL, D_MODEL)),
        const((3, D_MODEL)),
        const((D_MODEL, D_MODEL)),
        const((D_MODEL, D_MODEL)),
        const((1, D_MODEL)), const((1, D_MODEL)),
        const((D_MODEL, 2 * LANES)), const((D_MODEL, LANES)),
    ]
    out_specs = [
        pl.BlockSpec((TM, D_MODEL), lambda i: (i, 0)),
        pl.BlockSpec((TM, D_MODEL + GATE_COLS), lambda i: (i, 0)),
        pl.BlockSpec((N_EXPERTS, TM), lambda i: (0, i)),
    ]
    out_shape = [
        jax.ShapeDtypeStruct((n_tok, D_MODEL), F32),
        jax.ShapeDtypeStruct((n_tok, D_MODEL + GATE_COLS), BF16),
        jax.ShapeDtypeStruct((N_EXPERTS, n_tok), F32),
    ]
    return pl.pallas_call(
        functools.partial(_mixer_kernel, seq_len),
        grid=(n_tiles,),
        in_specs=in_specs,
        out_specs=out_specs,
        out_shape=out_shape,
        scratch_shapes=[pltpu.VMEM((TM + 2 * HALO, D_MODEL), F32),
                        pltpu.VMEM((TM + 2 * HALO, D_MODEL), F32)],
        compiler_params=pltpu.CompilerParams(dimension_semantics=("arbitrary",),
                                             vmem_limit_bytes=VMEM_LIMIT),
        name="mixer",
    )(x, x, x, p["ln_in_g"], p["ln_in_b"], p["w_in"], p["pool_w"], p["pool_scale"], p["w_pool_proj"],
      p["conv_w"], p["w_conv_out"], p["w_o"], p["ln1_g"], p["ln1_b"], p["wr_cat"], p["wr_hi"])


def _plan_kernel(cap, n_tiles, afft_ref, pos_ref, post_ref, starts_ref, sel_ref, eq_ref):
    n_tok = afft_ref.shape[1]
    bits = pltpu.bitcast(afft_ref[...], jnp.int32)

    def count(mask):
        return jnp.sum(jnp.where(mask, 1.0, 0.0), axis=1, keepdims=True)

    def bit_step(it, thr):
        cand = jnp.bitwise_or(thr, jnp.left_shift(jnp.int32(1), 30 - it))
        return jnp.where(count(bits >= cand) >= cap, cand, thr)

    thr = lax.fori_loop(0, 31, bit_step, jnp.zeros((N_EXPERTS, 1), jnp.int32))
    gt = bits > thr
    eq = bits == thr
    need = cap - count(gt)
    excess = jnp.max(count(eq) - need)
    sel_ref[...] = jnp.where(gt, 1.0, 0.0)
    eq_ref[...] = jnp.where(eq, 1.0, 0.0)
    r = lax.broadcasted_iota(jnp.int32, (LANES, LANES), 0)
    c = lax.broadcasted_iota(jnp.int32, (LANES, LANES), 1)

    @pl.when(excess <= 0.0)
    def _():
        sel_ref[...] = sel_ref[...] + eq_ref[...]

    @pl.when(excess > 0.0)
    def _():
        tri = _onehot(r <= c)

        def chunk(ci, seen):
            off = pl.multiple_of(ci * LANES, LANES)
            e_blk = eq_ref[:, pl.ds(off, LANES)]
            incl = _dot(e_blk.astype(BF16), tri)
            rank = seen + incl - e_blk
            sel_ref[:, pl.ds(off, LANES)] = sel_ref[:, pl.ds(off, LANES)] + jnp.where(rank < need, e_blk, 0.0)
            return seen + incl[:, LANES - 1:LANES]

        lax.fori_loop(0, n_tok // LANES, chunk, jnp.zeros((N_EXPERTS, 1), F32))

    lane = lax.broadcasted_iota(jnp.int32, (N_EXPERTS, LANES), 1)
    before = _onehot(r < c)
    unselected = jnp.full((LANES - N_EXPERTS, TB), -1.0, F32)
    run = jnp.zeros((N_EXPERTS, 1), F32)
    starts = jnp.zeros((N_EXPERTS, LANES), F32)
    for j in range(n_tiles):
        starts = jnp.where(lane == j, run, starts)
        base = run - BF16_ROWS * jnp.floor(run * (1.0 / BF16_ROWS))
        tile_pos = []
        for c0 in range(j * TB, (j + 1) * TB, LANES):
            blk = sel_ref[:, c0:c0 + LANES]
            tile_pos.append(jnp.where(blk > 0.5, _dot(blk.astype(BF16), before) + base, -1.0))
            total = jnp.sum(blk, axis=1, keepdims=True)
            base = base + total
            run = run + total
        tile_pos = jnp.concatenate(tile_pos, axis=1)
        pos_ref[:, j * TB:(j + 1) * TB] = tile_pos
        post_ref[j * TB:(j + 1) * TB, :] = jnp.concatenate([tile_pos, unselected], axis=0).T
    starts_ref[...] = starts.astype(jnp.int32)


def _plan(afft, cap):
    n_tok = afft.shape[1]
    n_tiles = n_tok // TB
    assert n_tiles <= LANES
    return pl.pallas_call(
        functools.partial(_plan_kernel, cap, n_tiles),
        out_shape=[jax.ShapeDtypeStruct((N_EXPERTS, n_tok), F32),
                   jax.ShapeDtypeStruct((n_tok, LANES), F32),
                   jax.ShapeDtypeStruct((N_EXPERTS, LANES), jnp.int32)],
        scratch_shapes=[pltpu.VMEM((N_EXPERTS, n_tok), F32), pltpu.VMEM((N_EXPERTS, n_tok), F32)],
        compiler_params=pltpu.CompilerParams(vmem_limit_bytes=VMEM_LIMIT),
        name="plan",
    )(afft)


def _expert_scalars(starts_ref, j, e):
    st = starts_ref[j, e]
    a = jnp.bitwise_and(st, -BF16_ROWS)
    return a, st - a, starts_ref[j + 1, e] - st


def _tile_scalars(starts_ref, j):
    return [_expert_scalars(starts_ref, j, e) for e in range(N_EXPERTS)]


def _passes(off, n):
    return lax.div(off + n + (CBW - 1), CBW)


def _max_passes(scal):
    last = scal[0][1] + scal[0][2]
    for _, off, n in scal[1:]:
        last = jnp.maximum(last, off + n)
    return _passes(last, 0)


def _window(a, k):
    return pl.ds(pl.multiple_of(a + k * CBW, BF16_ROWS), CBW)


def _dispatch_kernel(starts_ref, pos_ref, xext_ref, xe_hbm, gates_hbm,
                     wx, wg, ox, og, carry_x, carry_g, sem, osem):
    j = pl.program_id(0)
    last_step = pl.num_programs(0) - 1
    slot = lax.rem(j, 2)

    @pl.when(j == 0)
    def _():
        carry_x[...] = jnp.zeros_like(carry_x)
        carry_g[...] = jnp.zeros_like(carry_g)
        wx[...] = jnp.zeros_like(wx)
        wg[...] = jnp.zeros_like(wg)
        ox[...] = jnp.zeros_like(ox)
        og[...] = jnp.zeros_like(og)

    def copies(s, e, a):
        return (pltpu.make_async_copy(wx.at[s, e, pl.ds(0, CBW)], xe_hbm.at[e, _window(a, 0)], sem.at[0, s]),
                pltpu.make_async_copy(wg.at[s, e, pl.ds(0, CBW)], gates_hbm.at[e, _window(a, 0)], sem.at[1, s]))

    def keep_partial_group(e, a, off, n, k, bx, bg):
        @pl.when(off + n <= (k + 1) * CBW)
        def _():
            d = pl.multiple_of(jnp.bitwise_and(a + off + n, -BF16_ROWS) - a - k * CBW, BF16_ROWS)
            carry_x[e] = bx[pl.ds(d, BF16_ROWS), :]
            carry_g[e] = bg[pl.ds(d, BF16_ROWS), :]

    scal = _tile_scalars(starts_ref, j)
    pos = pos_ref[...]
    row = lax.broadcasted_iota(jnp.int32, (CBW, 1), 0).astype(F32)
    w = _dot(jnp.concatenate([_onehot(pos[e:e + 1, :] == row) for e in range(N_EXPERTS)], axis=0),
             xext_ref[...])
    for e, (a, off, n) in enumerate(scal):
        we = w[e * CBW:(e + 1) * CBW]
        head = slice(0, BF16_ROWS)
        wx[slot, e, 0:CBW, :] = we[:, :D_MODEL].astype(BF16)
        wg[slot, e, 0:CBW, :] = we[:, D_MODEL:]
        wx[slot, e, head, :] = (wx[slot, e, head, :].astype(F32) + carry_x[e].astype(F32)).astype(BF16)
        wg[slot, e, head, :] = wg[slot, e, head, :] + carry_g[e]
        keep_partial_group(e, a, off, n, 0, wx.at[slot, e], wg.at[slot, e])

    @pl.when(j > 0)
    def _():
        for e in range(N_EXPERTS):
            a, _, _ = _expert_scalars(starts_ref, j - 1, e)
            for cp in copies(1 - slot, e, a):
                cp.wait()

    for e, (a, _, _) in enumerate(scal):
        for cp in copies(slot, e, a):
            cp.start()

    @pl.when(_max_passes(scal) > 1)
    def _():
        def expert(e, carry):
            a, off, n = _expert_scalars(starts_ref, j, e)
            prow = pos_ref[pl.ds(e, 1), :]

            def extra(k, carry2):
                wv = _dot(_onehot(prow == row + (k * CBW).astype(F32)), xext_ref[...])
                ox[0:CBW, :] = wv[:, :D_MODEL].astype(BF16)
                og[0:CBW, :] = wv[:, D_MODEL:]
                cps = (pltpu.make_async_copy(ox.at[pl.ds(0, CBW)], xe_hbm.at[e, _window(a, k)], osem.at[0]),
                       pltpu.make_async_copy(og.at[pl.ds(0, CBW)], gates_hbm.at[e, _window(a, k)], osem.at[1]))
                for cp in cps:
                    cp.start()
                keep_partial_group(e, a, off, n, k, ox, og)
                for cp in cps:
                    cp.wait()
                return carry2

            return lax.fori_loop(1, _passes(off, n), extra, carry)

        lax.fori_loop(0, N_EXPERTS, expert, 0)

    @pl.when(j == last_step)
    def _():
        for e, (a, _, _) in enumerate(scal):
            for cp in copies(slot, e, a):
                cp.wait()
        ox[0:CBW, :] = jnp.zeros((CBW, D_MODEL), BF16)
        og[0:CBW, :] = jnp.zeros((CBW, GATE_COLS), F32)
        tail = pl.ds(xe_hbm.shape[1] - CBW, CBW)
        pads = [cp for e in range(N_EXPERTS) for cp in
                (pltpu.make_async_copy(ox.at[pl.ds(0, CBW)], xe_hbm.at[e, tail], osem.at[0]),
                 pltpu.make_async_copy(og.at[pl.ds(0, CBW)], gates_hbm.at[e, tail], osem.at[1]))]
        for cp in pads:
            cp.start()
        for cp in pads:
            cp.wait()


def _dispatch(starts_tab, pos, xext, cap):
    n_tok = pos.shape[1]
    n_tiles = n_tok // TB
    rows = cap + CBW
    grid_spec = pltpu.PrefetchScalarGridSpec(
        num_scalar_prefetch=1,
        grid=(n_tiles,),
        in_specs=[pl.BlockSpec((N_EXPERTS, TB), lambda j, s: (0, j)),
                  pl.BlockSpec((TB, D_MODEL + GATE_COLS), lambda j, s: (j, 0))],
        out_specs=[pl.BlockSpec(memory_space=pl.ANY), pl.BlockSpec(memory_space=pl.ANY)],
        scratch_shapes=[pltpu.VMEM((2, N_EXPERTS, CBW + BF16_ROWS, D_MODEL), BF16),
                        pltpu.VMEM((2, N_EXPERTS, CBW + BF16_ROWS, GATE_COLS), F32),
                        pltpu.VMEM((CBW + BF16_ROWS, D_MODEL), BF16),
                        pltpu.VMEM((CBW + BF16_ROWS, GATE_COLS), F32),
                        pltpu.VMEM((N_EXPERTS, BF16_ROWS, D_MODEL), BF16),
                        pltpu.VMEM((N_EXPERTS, BF16_ROWS, GATE_COLS), F32),
                        pltpu.SemaphoreType.DMA((2, 2)),
                        pltpu.SemaphoreType.DMA((2,))],
    )
    return pl.pallas_call(
        _dispatch_kernel,
        grid_spec=grid_spec,
        out_shape=[jax.ShapeDtypeStruct((N_EXPERTS, rows, D_MODEL), BF16),
                   jax.ShapeDtypeStruct((N_EXPERTS, rows, GATE_COLS), F32)],
        compiler_params=pltpu.CompilerParams(dimension_semantics=("arbitrary",),
                                             vmem_limit_bytes=VMEM_LIMIT),
        name="dispatch",
    )(starts_tab, pos, xext)


def _ffn_kernel(cap, xe_ref, gates_ref, wg_ref, wu_ref, wd_ref, ye_ref, wg_b, wu_b, wd_b):
    e = pl.program_id(0)
    wg_b[...] = wg_ref[0].astype(BF16)
    wu_b[...] = wu_ref[0].astype(BF16)
    wd_b[...] = wd_ref[0].astype(BF16)
    lane = lax.broadcasted_iota(jnp.int32, (FFN_ROWS, GATE_COLS), 1)
    mine = jnp.logical_and(jnp.bitwise_and(lane, N_EXPERTS - 1) == e, lane < 3 * N_EXPERTS)
    for c0 in range(0, cap, FFN_ROWS):
        rows = slice(c0, c0 + FFN_ROWS)
        x = xe_ref[0, rows, :]
        act = jax.nn.silu(_dot(x, wg_b[...])) * _dot(x, wu_b[...])
        y = _dot(act.astype(BF16), wd_b[...])
        gate = jnp.sum(jnp.where(mine, gates_ref[0, rows, :], 0.0), axis=1, keepdims=True)
        ye_ref[0, rows, :] = (y * gate).astype(BF16)
    ye_ref[0, cap:, :] = jnp.zeros((ye_ref.shape[1] - cap, D_MODEL), BF16)


def _ffn(xe, gates, w_gate, w_up, w_down, cap):
    rows = xe.shape[1]
    assert cap % FFN_ROWS == 0
    wspec = pl.BlockSpec((1, D_MODEL, D_MODEL), lambda e: (e, 0, 0))
    return pl.pallas_call(
        functools.partial(_ffn_kernel, cap),
        grid=(N_EXPERTS,),
        in_specs=[pl.BlockSpec((1, rows, D_MODEL), lambda e: (e, 0, 0)),
                  pl.BlockSpec((1, rows, GATE_COLS), lambda e: (e, 0, 0)),
                  wspec, wspec, wspec],
        out_specs=pl.BlockSpec((1, rows, D_MODEL), lambda e: (e, 0, 0)),
        out_shape=jax.ShapeDtypeStruct((N_EXPERTS, rows, D_MODEL), BF16),
        scratch_shapes=[pltpu.VMEM((D_MODEL, D_MODEL), BF16)] * 3,
        compiler_params=pltpu.CompilerParams(dimension_semantics=("arbitrary",),
                                             vmem_limit_bytes=VMEM_LIMIT),
        name="ffn",
    )(xe, gates, w_gate, w_up, w_down)


def _combine_kernel(starts_ref, post_ref, x1_ref, ln2g_ref, ln2b_ref, ye_hbm, out_ref, ybuf, obuf, sem, osem):
    j = pl.program_id(0)
    n_steps = pl.num_programs(0)
    slot = lax.rem(j, 2)

    def fetch(s, step):
        return [pltpu.make_async_copy(ye_hbm.at[e, _window(_expert_scalars(starts_ref, step, e)[0], 0)],
                                      ybuf.at[s, e], sem.at[s]) for e in range(N_EXPERTS)]

    @pl.when(j == 0)
    def _():
        for cp in fetch(0, 0):
            cp.start()

    @pl.when(j + 1 < n_steps)
    def _():
        for cp in fetch(1 - slot, j + 1):
            cp.start()

    pos = post_ref[...]
    col = lax.broadcasted_iota(jnp.int32, (1, CBW), 1).astype(F32)
    scatter = jnp.concatenate([_onehot(pos[:, e:e + 1] == col) for e in range(N_EXPERTS)], axis=1)
    for cp in fetch(slot, j):
        cp.wait()
    out_ref[...] = _dot(scatter, ybuf[slot].reshape(N_EXPERTS * CBW, D_MODEL))

    @pl.when(_max_passes(_tile_scalars(starts_ref, j)) > 1)
    def _():
        lane = lax.broadcasted_iota(jnp.int32, (1, LANES), 1)

        def expert(e, carry):
            a, off, n = _expert_scalars(starts_ref, j, e)
            pcol = jnp.sum(jnp.where(lane == e, pos, 0.0), axis=1, keepdims=True)

            def extra(k, carry2):
                cp = pltpu.make_async_copy(ye_hbm.at[e, _window(a, k)], obuf, osem.at[0])
                cp.start()
                cp.wait()
                out_ref[...] += _dot(_onehot(pcol == col + (k * CBW).astype(F32)), obuf[...])
                return carry2

            return lax.fori_loop(1, _passes(off, n), extra, carry)

        lax.fori_loop(0, N_EXPERTS, expert, 0)

    out_ref[...] = _layer_norm(ALPHA * x1_ref[...] + out_ref[...], ln2g_ref[...], ln2b_ref[...])


def _combine(starts_tab, post, x1, ye, ln2_g, ln2_b):
    n_tok = post.shape[0]
    n_tiles = n_tok // TB
    grid_spec = pltpu.PrefetchScalarGridSpec(
        num_scalar_prefetch=1,
        grid=(n_tiles,),
        in_specs=[pl.BlockSpec((TB, LANES), lambda j, s: (j, 0)),
                  pl.BlockSpec((TB, D_MODEL), lambda j, s: (j, 0)),
                  pl.BlockSpec((1, D_MODEL), lambda j, s: (0, 0)),
                  pl.BlockSpec((1, D_MODEL), lambda j, s: (0, 0)),
                  pl.BlockSpec(memory_space=pl.ANY)],
        out_specs=pl.BlockSpec((TB, D_MODEL), lambda j, s: (j, 0)),
        scratch_shapes=[pltpu.VMEM((2, N_EXPERTS, CBW, D_MODEL), BF16),
                        pltpu.VMEM((CBW, D_MODEL), BF16),
                        pltpu.SemaphoreType.DMA((2,)),
                        pltpu.SemaphoreType.DMA((1,))],
    )
    return pl.pallas_call(
        _combine_kernel,
        grid_spec=grid_spec,
        out_shape=jax.ShapeDtypeStruct((n_tok, D_MODEL), F32),
        compiler_params=pltpu.CompilerParams(dimension_semantics=("arbitrary",),
                                             vmem_limit_bytes=VMEM_LIMIT),
        name="combine",
    )(starts_tab, post, x1, ln2_g, ln2_b, ye)


def _trunk(x, p):
    b, s, d = x.shape
    n_tok = b * s
    cap = EC_CAPACITY_FACTOR * n_tok // N_EXPERTS
    n_tiles = n_tok // TB
    x1, xext, afft = _mixer(x.reshape(n_tok, d), s, p)
    pos, post, starts = _plan(afft, cap)
    starts_tab = jnp.concatenate([starts[:, :n_tiles].T, jnp.full((1, N_EXPERTS), cap, jnp.int32)], axis=0)
    xe, gates = _dispatch(starts_tab, pos, xext, cap)
    ye = _ffn(xe, gates, p["w_gate"], p["w_up"], p["w_down"], cap)
    y = _combine(starts_tab, post, x1, ye, p["ln2_g"], p["ln2_b"])
    return y.reshape(b, s, d)


def kernel(x_prompt, x_sample, ln_in_g, ln_in_b, w_in, pool_w, pool_scale, w_pool_proj, conv_w, w_conv_out, w_o, ln1_g, ln1_b, w_router, w_gate, w_up, w_down, ln2_g, ln2_b):
    assert w_in.shape == (1, D_MODEL, IN_COLS)
    wr = jnp.pad(w_router[0], ((0, 0), (0, LANES - N_EXPERTS)))
    wr_hi = wr.astype(BF16)
    wr_lo = (wr - wr_hi.astype(F32)).astype(BF16)
    p = {
        "ln_in_g": ln_in_g.reshape(1, D_MODEL), "ln_in_b": ln_in_b.reshape(1, D_MODEL),
        "w_in": w_in[0].astype(BF16), "pool_w": pool_w[0].astype(BF16), "pool_scale": pool_scale,
        "w_pool_proj": w_pool_proj[0].astype(BF16), "conv_w": conv_w[0],
        "w_conv_out": w_conv_out[0].astype(BF16), "w_o": w_o[0].astype(BF16),
        "ln1_g": ln1_g, "ln1_b": ln1_b,
        "wr_cat": jnp.concatenate([wr_hi, wr_lo], axis=1), "wr_hi": wr_hi,
        "w_gate": w_gate[0], "w_up": w_up[0], "w_down": w_down[0],
        "ln2_g": ln2_g, "ln2_b": ln2_b,
    }
    return _trunk(x_prompt, p), _trunk(x_sample, p)
```

```python
import functools

import jax
import jax.numpy as jnp
from jax import lax
from jax.experimental import pallas as pl
from jax.experimental.pallas import tpu as pltpu

F32 = jnp.float32
BF16 = jnp.bfloat16

D_MODEL = 1024
N_EXPERTS = 16
EC_CAPACITY_FACTOR = 2
POOL_WINDOWS = (2, 4, 8, 16)
POOL_GROUP = D_MODEL // len(POOL_WINDOWS)
ALPHA = 2.0 ** 0.25
LN_EPS = 1e-5
IN_COLS = 6 * D_MODEL

LANES = 128
SUBLANES = 8
BF16_ROWS = 16
TM = 512
HALO = SUBLANES
TB = 512
CBW = 128
FFN_ROWS = 512
GATE_COLS = LANES
VMEM_LIMIT = 56 * 1024 * 1024


def _layer_norm(x, g, b):
    mu = jnp.mean(x, axis=-1, keepdims=True)
    xc = x - mu
    var = jnp.mean(xc * xc, axis=-1, keepdims=True)
    return xc * lax.rsqrt(var + LN_EPS) * g + b


def _dot(a, b):
    return jnp.dot(a, b, preferred_element_type=F32)


def _onehot(mask):
    return jnp.where(mask, 1.0, 0.0).astype(BF16)


def _mixer_kernel(seq_len, x_ref, xprev_ref, xnext_ref, lng_ref, lnb_ref, win_ref, poolw_ref,
                  pscale_ref, wpp_ref, convw_ref, wco_ref, wo_ref, ln1g_ref, ln1b_ref,
                  wrcat_ref, wrhi_ref, x1_ref, xext_ref, afft_ref, u_ext, z_ext):
    i = pl.program_id(0)
    p0 = lax.rem(i * TM, seq_len)
    rows = TM + 2 * HALO
    main = slice(HALO, HALO + TM)

    xa = jnp.concatenate([xprev_ref[...], x_ref[...], xnext_ref[...]], axis=0)
    xna = _layer_norm(xa, lng_ref[...], lnb_ref[...])
    xb = xna.astype(BF16)
    xn = xna[main]
    xb_main = xn.astype(BF16)

    ridx = lax.broadcasted_iota(jnp.int32, (rows, 1), 0)
    lo_keep = jnp.where(p0 == 0, HALO, 0)
    hi_keep = jnp.where(p0 + TM == seq_len, HALO + TM, rows)
    keep = jnp.logical_and(ridx >= lo_keep, ridx < hi_keep)

    def proj(lhs, k):
        return _dot(lhs, win_ref[:, k * D_MODEL:(k + 1) * D_MODEL])

    u_ext[...] = jnp.where(keep, proj(xb, 0), 0.0)
    z_ext[...] = jnp.where(keep, proj(xb, 3) * proj(xb, 1), 0.0)

    pos = p0 + lax.broadcasted_iota(jnp.int32, (TM, 1), 0)
    mixed = []
    for gi, w in enumerate(POOL_WINDOWS):
        cols = slice(gi * POOL_GROUP, (gi + 1) * POOL_GROUP)
        s = None
        for j in range(-(w // 2), w - w // 2):
            v = u_ext[HALO + j:HALO + j + TM, cols]
            s = v if s is None else s + v
        cnt = (jnp.minimum(pos + (w - w // 2), seq_len) - jnp.maximum(pos - w // 2, 0)).astype(F32)
        pooled = s / cnt - u_ext[main, cols]
        mixed.append(_dot(pooled.astype(BF16), poolw_ref[gi]))
    mixed = jnp.concatenate(mixed, axis=1) * pscale_ref[...]
    y_pool = _dot(mixed.astype(BF16), wpp_ref[...])

    cw = convw_ref[...]
    conv = (z_ext[HALO - 1:HALO - 1 + TM, :] * cw[0:1, :] + z_ext[main, :] * cw[1:2, :]
            + z_ext[HALO + 1:HALO + 1 + TM, :] * cw[2:3, :])
    y_conv = _dot((proj(xb_main, 2) * conv).astype(BF16), wco_ref[...])

    merged = jax.nn.sigmoid(proj(xb_main, 4)) * y_pool + jax.nn.sigmoid(proj(xb_main, 5)) * y_conv
    m = _dot(merged.astype(BF16), wo_ref[...])
    x1 = _layer_norm(ALPHA * xn + m, ln1g_ref[...], ln1b_ref[...])
    x1_ref[...] = x1

    x1h = x1.astype(BF16)
    x1l = (x1 - x1h.astype(F32)).astype(BF16)
    lg = _dot(x1h, wrcat_ref[...])
    logits = lg[:, :LANES] + lg[:, LANES:] + _dot(x1l, wrhi_ref[...])
    lane = lax.broadcasted_iota(jnp.int32, (TM, LANES), 1)
    logits = jnp.where(lane < N_EXPERTS, logits, -1e30)
    ex = jnp.exp(logits - jnp.max(logits, axis=1, keepdims=True))
    aff = ex / jnp.sum(ex, axis=1, keepdims=True)
    afft_ref[...] = aff.T[:N_EXPERTS, :]

    a1 = aff.astype(BF16).astype(F32)
    r1 = aff - a1
    a2 = r1.astype(BF16).astype(F32)
    a3 = (r1 - a2).astype(BF16).astype(F32)
    ext = a1 + pltpu.roll(a2, N_EXPERTS, 1) + pltpu.roll(a3, 2 * N_EXPERTS, 1)
    xext_ref[:, :D_MODEL] = x1h
    xext_ref[:, D_MODEL:] = ext.astype(BF16)


def _mixer(x, seq_len, p):
    n_tok = x.shape[0]
    n_tiles = n_tok // TM
    assert n_tok % TM == 0 and seq_len % TM == 0
    tiles_per_halo = TM // HALO
    last_halo = n_tok // HALO - 1

    def const(shape):
        return pl.BlockSpec(shape, lambda i: (0,) * len(shape), pipeline_mode=pl.Buffered(1))

    in_specs = [
        pl.BlockSpec((TM, D_MODEL), lambda i: (i, 0)),
        pl.BlockSpec((HALO, D_MODEL), lambda i: (jnp.maximum(i * tiles_per_halo - 1, 0), 0)),
        pl.BlockSpec((HALO, D_MODEL), lambda i: (jnp.minimum((i + 1) * tiles_per_halo, last_halo), 0)),
        const((1, D_MODEL)), const((1, D_MODEL)),
        const((D_MODEL, IN_COLS)),
        const((len(POOL_WINDOWS), POOL_GROUP, POOL_GROUP)),
        const((1, D_MODEL)),
        const((D_MODEL, D_MODEL)),
        const((3, D_MODEL)),
        const((D_MODEL, D_MODEL)),
        const((D_MODEL, D_MODEL)),
        const((1, D_MODEL)), const((1, D_MODEL)),
        const((D_MODEL, 2 * LANES)), const((D_MODEL, LANES)),
    ]
    out_specs = [
        pl.BlockSpec((TM, D_MODEL), lambda i: (i, 0)),
        pl.BlockSpec((TM, D_MODEL + GATE_COLS), lambda i: (i, 0)),
        pl.BlockSpec((N_EXPERTS, TM), lambda i: (0, i)),
    ]
    out_shape = [
        jax.ShapeDtypeStruct((n_tok, D_MODEL), F32),
        jax.ShapeDtypeStruct((n_tok, D_MODEL + GATE_COLS), BF16),
        jax.ShapeDtypeStruct((N_EXPERTS, n_tok), F32),
    ]
    return pl.pallas_call(
        functools.partial(_mixer_kernel, seq_len),
        grid=(n_tiles,),
        in_specs=in_specs,
        out_specs=out_specs,
        out_shape=out_shape,
        scratch_shapes=[pltpu.VMEM((TM + 2 * HALO, D_MODEL), F32),
                        pltpu.VMEM((TM + 2 * HALO, D_MODEL), F32)],
        compiler_params=pltpu.CompilerParams(dimension_semantics=("arbitrary",),
                                             vmem_limit_bytes=VMEM_LIMIT),
        name="mixer",
    )(x, x, x, p["ln_in_g"], p["ln_in_b"], p["w_in"], p["pool_w"], p["pool_scale"], p["w_pool_proj"],
      p["conv_w"], p["w_conv_out"], p["w_o"], p["ln1_g"], p["ln1_b"], p["wr_cat"], p["wr_hi"])


def _plan_kernel(cap, n_tiles, afft_ref, pos_ref, post_ref, starts_ref, sel_ref, eq_ref):
    n_tok = afft_ref.shape[1]
    bits = pltpu.bitcast(afft_ref[...], jnp.int32)

    def count(mask):
        return jnp.sum(jnp.where(mask, 1.0, 0.0), axis=1, keepdims=True)

    def bit_step(it, thr):
        cand = jnp.bitwise_or(thr, jnp.left_shift(jnp.int32(1), 30 - it))
        return jnp.where(count(bits >= cand) >= cap, cand, thr)

    thr = lax.fori_loop(0, 31, bit_step, jnp.zeros((N_EXPERTS, 1), jnp.int32))
    gt = bits > thr
    eq = bits == thr
    need = cap - count(gt)
    excess = jnp.max(count(eq) - need)
    sel_ref[...] = jnp.where(gt, 1.0, 0.0)
    eq_ref[...] = jnp.where(eq, 1.0, 0.0)
    r = lax.broadcasted_iota(jnp.int32, (LANES, LANES), 0)
    c = lax.broadcasted_iota(jnp.int32, (LANES, LANES), 1)

    @pl.when(excess <= 0.0)
    def _():
        sel_ref[...] = sel_ref[...] + eq_ref[...]

    @pl.when(excess > 0.0)
    def _():
        tri = _onehot(r <= c)

        def chunk(ci, seen):
            off = pl.multiple_of(ci * LANES, LANES)
            e_blk = eq_ref[:, pl.ds(off, LANES)]
            incl = _dot(e_blk.astype(BF16), tri)
            rank = seen + incl - e_blk
            sel_ref[:, pl.ds(off, LANES)] = sel_ref[:, pl.ds(off, LANES)] + jnp.where(rank < need, e_blk, 0.0)
            return seen + incl[:, LANES - 1:LANES]

        lax.fori_loop(0, n_tok // LANES, chunk, jnp.zeros((N_EXPERTS, 1), F32))

    lane = lax.broadcasted_iota(jnp.int32, (N_EXPERTS, LANES), 1)
    before = _onehot(r < c)
    unselected = jnp.full((LANES - N_EXPERTS, TB), -1.0, F32)
    run = jnp.zeros((N_EXPERTS, 1), F32)
    starts = jnp.zeros((N_EXPERTS, LANES), F32)
    for j in range(n_tiles):
        starts = jnp.where(lane == j, run, starts)
        base = run - BF16_ROWS * jnp.floor(run * (1.0 / BF16_ROWS))
        tile_pos = []
        for c0 in range(j * TB, (j + 1) * TB, LANES):
            blk = sel_ref[:, c0:c0 + LANES]
            tile_pos.append(jnp.where(blk > 0.5, _dot(blk.astype(BF16), before) + base, -1.0))
            total = jnp.sum(blk, axis=1, keepdims=True)
            base = base + total
            run = run + total
        tile_pos = jnp.concatenate(tile_pos, axis=1)
        pos_ref[:, j * TB:(j + 1) * TB] = tile_pos
        post_ref[j * TB:(j + 1) * TB, :] = jnp.concatenate([tile_pos, unselected], axis=0).T
    starts_ref[...] = starts.astype(jnp.int32)


def _plan(afft, cap):
    n_tok = afft.shape[1]
    n_tiles = n_tok // TB
    assert n_tiles <= LANES
    return pl.pallas_call(
        functools.partial(_plan_kernel, cap, n_tiles),
        out_shape=[jax.ShapeDtypeStruct((N_EXPERTS, n_tok), F32),
                   jax.ShapeDtypeStruct((n_tok, LANES), F32),
                   jax.ShapeDtypeStruct((N_EXPERTS, LANES), jnp.int32)],
        scratch_shapes=[pltpu.VMEM((N_EXPERTS, n_tok), F32), pltpu.VMEM((N_EXPERTS, n_tok), F32)],
        compiler_params=pltpu.CompilerParams(vmem_limit_bytes=VMEM_LIMIT),
        name="plan",
    )(afft)


def _expert_scalars(starts_ref, j, e):
    st = starts_ref[j, e]
    a = jnp.bitwise_and(st, -BF16_ROWS)
    return a, st - a, starts_ref[j + 1, e] - st


def _tile_scalars(starts_ref, j):
    return [_expert_scalars(starts_ref, j, e) for e in range(N_EXPERTS)]


def _passes(off, n):
    return lax.div(off + n + (CBW - 1), CBW)


def _max_passes(scal):
    last = scal[0][1] + scal[0][2]
    for _, off, n in scal[1:]:
        last = jnp.maximum(last, off + n)
    return _passes(last, 0)


def _window(a, k):
    return pl.ds(pl.multiple_of(a + k * CBW, BF16_ROWS), CBW)


def _gather_onehot(pos):
    row = lax.broadcasted_iota(jnp.int32, (CBW, 1), 0).astype(F32)
    return jnp.concatenate([_onehot(pos[e:e + 1, :] == row) for e in range(N_EXPERTS)], axis=0)


def _dispatch_kernel(starts_ref, pos_ref, pos_next_ref, xext_ref, xe_hbm, gates_hbm,
                     wx0, wx1, wg0, wg1, hot0, hot1, ox, og, carry_x, carry_g, sem, osem):
    j = pl.program_id(0)
    last_step = pl.num_programs(0) - 1
    slot = lax.rem(j, 2)
    wxs, wgs, hots = (wx0, wx1), (wg0, wg1), (hot0, hot1)
    row = lax.broadcasted_iota(jnp.int32, (CBW, 1), 0).astype(F32)

    @pl.when(j == 0)
    def _():
        carry_x[...] = jnp.zeros_like(carry_x)
        carry_g[...] = jnp.zeros_like(carry_g)
        for buf in (wx0, wx1, wg0, wg1, ox, og):
            buf[...] = jnp.zeros_like(buf)
        hot0[...] = _gather_onehot(pos_ref[...])

    def copies(s, e, a):
        return (pltpu.make_async_copy(wxs[s].at[e, pl.ds(0, CBW)], xe_hbm.at[e, _window(a, 0)], sem.at[0, s]),
                pltpu.make_async_copy(wgs[s].at[e, pl.ds(0, CBW)], gates_hbm.at[e, _window(a, 0)], sem.at[1, s]))

    def keep_partial_group(e, a, off, n, k, bx, bg):
        @pl.when(off + n <= (k + 1) * CBW)
        def _():
            d = pl.multiple_of(jnp.bitwise_and(a + off + n, -BF16_ROWS) - a - k * CBW, BF16_ROWS)
            carry_x[e] = bx[pl.ds(d, BF16_ROWS), :]
            carry_g[e] = bg[pl.ds(d, BF16_ROWS), :]

    scal = _tile_scalars(starts_ref, j)

    def first_windows(cur, nxt):
        wx, wg = wxs[cur], wgs[cur]
        w = _dot(hots[cur][...], xext_ref[...])
        hots[nxt][...] = _gather_onehot(pos_next_ref[...])
        head = slice(0, BF16_ROWS)
        for e in range(N_EXPERTS):
            we = w[e * CBW:(e + 1) * CBW]
            wx[e, 0:CBW, :] = we[:, :D_MODEL].astype(BF16)
            wg[e, 0:CBW, :] = we[:, D_MODEL:]
            wx[e, head, :] = (wx[e, head, :].astype(F32) + carry_x[e].astype(F32)).astype(BF16)
            wg[e, head, :] = wg[e, head, :] + carry_g[e]
        for e, (a, off, n) in enumerate(scal):
            keep_partial_group(e, a, off, n, 0, wx.at[e], wg.at[e])

        @pl.when(j > 0)
        def _():
            for e in range(N_EXPERTS):
                a, _, _ = _expert_scalars(starts_ref, j - 1, e)
                for cp in copies(nxt, e, a):
                    cp.wait()

        for e, (a, _, _) in enumerate(scal):
            for cp in copies(cur, e, a):
                cp.start()

        @pl.when(j == last_step)
        def _():
            for e, (a, _, _) in enumerate(scal):
                for cp in copies(cur, e, a):
                    cp.wait()

    pl.when(slot == 0)(functools.partial(first_windows, 0, 1))
    pl.when(slot == 1)(functools.partial(first_windows, 1, 0))

    @pl.when(_max_passes(scal) > 1)
    def _():
        def expert(e, carry):
            a, off, n = _expert_scalars(starts_ref, j, e)
            prow = pos_ref[pl.ds(e, 1), :]

            def extra(k, carry2):
                wv = _dot(_onehot(prow == row + (k * CBW).astype(F32)), xext_ref[...])
                ox[0:CBW, :] = wv[:, :D_MODEL].astype(BF16)
                og[0:CBW, :] = wv[:, D_MODEL:]
                cps = (pltpu.make_async_copy(ox.at[pl.ds(0, CBW)], xe_hbm.at[e, _window(a, k)], osem.at[0]),
                       pltpu.make_async_copy(og.at[pl.ds(0, CBW)], gates_hbm.at[e, _window(a, k)], osem.at[1]))
                for cp in cps:
                    cp.start()
                keep_partial_group(e, a, off, n, k, ox, og)
                for cp in cps:
                    cp.wait()
                return carry2

            return lax.fori_loop(1, _passes(off, n), extra, carry)

        lax.fori_loop(0, N_EXPERTS, expert, 0)

    @pl.when(j == last_step)
    def _():
        ox[0:CBW, :] = jnp.zeros((CBW, D_MODEL), BF16)
        og[0:CBW, :] = jnp.zeros((CBW, GATE_COLS), F32)
        tail = pl.ds(xe_hbm.shape[1] - CBW, CBW)
        pads = [cp for e in range(N_EXPERTS) for cp in
                (pltpu.make_async_copy(ox.at[pl.ds(0, CBW)], xe_hbm.at[e, tail], osem.at[0]),
                 pltpu.make_async_copy(og.at[pl.ds(0, CBW)], gates_hbm.at[e, tail], osem.at[1]))]
        for cp in pads:
            cp.start()
        for cp in pads:
            cp.wait()


def _dispatch(starts_tab, pos, xext, cap):
    n_tok = pos.shape[1]
    n_tiles = n_tok // TB
    rows = cap + CBW
    grid_spec = pltpu.PrefetchScalarGridSpec(
        num_scalar_prefetch=1,
        grid=(n_tiles,),
        in_specs=[pl.BlockSpec((N_EXPERTS, TB), lambda j, s: (0, j)),
                  pl.BlockSpec((N_EXPERTS, TB), lambda j, s: (0, jnp.minimum(j + 1, n_tiles - 1))),
                  pl.BlockSpec((TB, D_MODEL + GATE_COLS), lambda j, s: (j, 0))],
        out_specs=[pl.BlockSpec(memory_space=pl.ANY), pl.BlockSpec(memory_space=pl.ANY)],
        scratch_shapes=[pltpu.VMEM((N_EXPERTS, CBW + BF16_ROWS, D_MODEL), BF16),
                        pltpu.VMEM((N_EXPERTS, CBW + BF16_ROWS, D_MODEL), BF16),
                        pltpu.VMEM((N_EXPERTS, CBW + BF16_ROWS, GATE_COLS), F32),
                        pltpu.VMEM((N_EXPERTS, CBW + BF16_ROWS, GATE_COLS), F32),
                        pltpu.VMEM((N_EXPERTS * CBW, TB), BF16),
                        pltpu.VMEM((N_EXPERTS * CBW, TB), BF16),
                        pltpu.VMEM((CBW + BF16_ROWS, D_MODEL), BF16),
                        pltpu.VMEM((CBW + BF16_ROWS, GATE_COLS), F32),
                        pltpu.VMEM((N_EXPERTS, BF16_ROWS, D_MODEL), BF16),
                        pltpu.VMEM((N_EXPERTS, BF16_ROWS, GATE_COLS), F32),
                        pltpu.SemaphoreType.DMA((2, 2)),
                        pltpu.SemaphoreType.DMA((2,))],
    )
    return pl.pallas_call(
        _dispatch_kernel,
        grid_spec=grid_spec,
        out_shape=[jax.ShapeDtypeStruct((N_EXPERTS, rows, D_MODEL), BF16),
                   jax.ShapeDtypeStruct((N_EXPERTS, rows, GATE_COLS), F32)],
        compiler_params=pltpu.CompilerParams(dimension_semantics=("arbitrary",),
                                             vmem_limit_bytes=VMEM_LIMIT),
        name="dispatch",
    )(starts_tab, pos, pos, xext)


def _ffn_kernel(cap, xe_ref, gates_ref, wg_ref, wu_ref, wd_ref, ye_ref, wg_b, wu_b, wd_b):
    e = pl.program_id(0)
    wg_b[...] = wg_ref[0].astype(BF16)
    wu_b[...] = wu_ref[0].astype(BF16)
    wd_b[...] = wd_ref[0].astype(BF16)
    lane = lax.broadcasted_iota(jnp.int32, (FFN_ROWS, GATE_COLS), 1)
    mine = jnp.logical_and(jnp.bitwise_and(lane, N_EXPERTS - 1) == e, lane < 3 * N_EXPERTS)
    for c0 in range(0, cap, FFN_ROWS):
        rows = slice(c0, c0 + FFN_ROWS)
        x = xe_ref[0, rows, :]
        act = jax.nn.silu(_dot(x, wg_b[...])) * _dot(x, wu_b[...])
        y = _dot(act.astype(BF16), wd_b[...])
        gate = jnp.sum(jnp.where(mine, gates_ref[0, rows, :], 0.0), axis=1, keepdims=True)
        ye_ref[0, rows, :] = (y * gate).astype(BF16)
    ye_ref[0, cap:, :] = jnp.zeros((ye_ref.shape[1] - cap, D_MODEL), BF16)


def _ffn(xe, gates, w_gate, w_up, w_down, cap):
    rows = xe.shape[1]
    assert cap % FFN_ROWS == 0
    wspec = pl.BlockSpec((1, D_MODEL, D_MODEL), lambda e: (e, 0, 0))
    return pl.pallas_call(
        functools.partial(_ffn_kernel, cap),
        grid=(N_EXPERTS,),
        in_specs=[pl.BlockSpec((1, rows, D_MODEL), lambda e: (e, 0, 0)),
                  pl.BlockSpec((1, rows, GATE_COLS), lambda e: (e, 0, 0)),
                  wspec, wspec, wspec],
        out_specs=pl.BlockSpec((1, rows, D_MODEL), lambda e: (e, 0, 0)),
        out_shape=jax.ShapeDtypeStruct((N_EXPERTS, rows, D_MODEL), BF16),
        scratch_shapes=[pltpu.VMEM((D_MODEL, D_MODEL), BF16)] * 3,
        compiler_params=pltpu.CompilerParams(dimension_semantics=("arbitrary",),
                                             vmem_limit_bytes=VMEM_LIMIT),
        name="ffn",
    )(xe, gates, w_gate, w_up, w_down)


def _scatter_onehot(pos):
    col = lax.broadcasted_iota(jnp.int32, (1, CBW), 1).astype(F32)
    return jnp.concatenate([_onehot(pos[:, e:e + 1] == col) for e in range(N_EXPERTS)], axis=1)


def _combine_kernel(starts_ref, post_ref, post_next_ref, x1_ref, ln2g_ref, ln2b_ref, ye_hbm, out_ref,
                    ybuf0, ybuf1, hot0, hot1, acc0, acc1, obuf, sem, osem):
    j = pl.program_id(0)
    n_tiles = pl.num_programs(0) - 1
    slot = lax.rem(j, 2)
    ybufs, hots, accs = (ybuf0, ybuf1), (hot0, hot1), (acc0, acc1)

    def fetch(s, tile):
        return [pltpu.make_async_copy(ye_hbm.at[e, _window(_expert_scalars(starts_ref, tile, e)[0], 0)],
                                      ybufs[s].at[e], sem.at[s]) for e in range(N_EXPERTS)]

    @pl.when(j == 0)
    def _():
        for cp in fetch(0, 0):
            cp.start()
        hot0[...] = _scatter_onehot(post_ref[...])
        acc1[...] = jnp.zeros_like(acc1)

    def step(cur, nxt):
        @pl.when(j + 1 < n_tiles)
        def _():
            for cp in fetch(nxt, j + 1):
                cp.start()

        @pl.when(j < n_tiles)
        def _():
            for cp in fetch(cur, j):
                cp.wait()

        accs[cur][...] = _dot(hots[cur][...], ybufs[cur][...].reshape(N_EXPERTS * CBW, D_MODEL))
        hots[nxt][...] = _scatter_onehot(post_next_ref[...])
        out_ref[...] = _layer_norm(ALPHA * x1_ref[...] + accs[nxt][...], ln2g_ref[...], ln2b_ref[...])

        tile = jnp.minimum(j, n_tiles - 1)

        @pl.when(jnp.logical_and(j < n_tiles, _max_passes(_tile_scalars(starts_ref, tile)) > 1))
        def _():
            lane = lax.broadcasted_iota(jnp.int32, (1, LANES), 1)
            col = lax.broadcasted_iota(jnp.int32, (1, CBW), 1).astype(F32)

            def expert(e, carry):
                a, off, n = _expert_scalars(starts_ref, tile, e)
                pcol = jnp.sum(jnp.where(lane == e, post_ref[...], 0.0), axis=1, keepdims=True)

                def extra(k, carry2):
                    cp = pltpu.make_async_copy(ye_hbm.at[e, _window(a, k)], obuf, osem.at[0])
                    cp.start()
                    cp.wait()
                    accs[cur][...] += _dot(_onehot(pcol == col + (k * CBW).astype(F32)), obuf[...])
                    return carry2

                return lax.fori_loop(1, _passes(off, n), extra, carry)

            lax.fori_loop(0, N_EXPERTS, expert, 0)

    pl.when(slot == 0)(functools.partial(step, 0, 1))
    pl.when(slot == 1)(functools.partial(step, 1, 0))


def _combine(starts_tab, post, x1, ye, ln2_g, ln2_b):
    n_tok = post.shape[0]
    n_tiles = n_tok // TB
    assert n_tiles >= 2
    last = n_tiles - 1
    grid_spec = pltpu.PrefetchScalarGridSpec(
        num_scalar_prefetch=1,
        grid=(n_tiles + 1,),
        in_specs=[pl.BlockSpec((TB, LANES), lambda j, s: (jnp.minimum(j, last), 0)),
                  pl.BlockSpec((TB, LANES), lambda j, s: (jnp.minimum(j + 1, last), 0)),
                  pl.BlockSpec((TB, D_MODEL), lambda j, s: (jnp.maximum(j - 1, 0), 0)),
                  pl.BlockSpec((1, D_MODEL), lambda j, s: (0, 0)),
                  pl.BlockSpec((1, D_MODEL), lambda j, s: (0, 0)),
                  pl.BlockSpec(memory_space=pl.ANY)],
        out_specs=pl.BlockSpec((TB, D_MODEL), lambda j, s: (jnp.maximum(j - 1, 0), 0)),
        scratch_shapes=[pltpu.VMEM((N_EXPERTS, CBW, D_MODEL), BF16),
                        pltpu.VMEM((N_EXPERTS, CBW, D_MODEL), BF16),
                        pltpu.VMEM((TB, N_EXPERTS * CBW), BF16),
                        pltpu.VMEM((TB, N_EXPERTS * CBW), BF16),
                        pltpu.VMEM((TB, D_MODEL), F32),
                        pltpu.VMEM((TB, D_MODEL), F32),
                        pltpu.VMEM((CBW, D_MODEL), BF16),
                        pltpu.SemaphoreType.DMA((2,)),
                        pltpu.SemaphoreType.DMA((1,))],
    )
    return pl.pallas_call(
        _combine_kernel,
        grid_spec=grid_spec,
        out_shape=jax.ShapeDtypeStruct((n_tok, D_MODEL), F32),
        compiler_params=pltpu.CompilerParams(dimension_semantics=("arbitrary",),
                                             vmem_limit_bytes=VMEM_LIMIT),
        name="combine",
    )(starts_tab, post, post, x1, ln2_g, ln2_b, ye)


def _trunk(x, p):
    b, s, d = x.shape
    n_tok = b * s
    cap = EC_CAPACITY_FACTOR * n_tok // N_EXPERTS
    n_tiles = n_tok // TB
    x1, xext, afft = _mixer(x.reshape(n_tok, d), s, p)
    pos, post, starts = _plan(afft, cap)
    starts_tab = jnp.concatenate([starts[:, :n_tiles].T, jnp.full((1, N_EXPERTS), cap, jnp.int32)], axis=0)
    xe, gates = _dispatch(starts_tab, pos, xext, cap)
    ye = _ffn(xe, gates, p["w_gate"], p["w_up"], p["w_down"], cap)
    y = _combine(starts_tab, post, x1, ye, p["ln2_g"], p["ln2_b"])
    return y.reshape(b, s, d)


def kernel(x_prompt, x_sample, ln_in_g, ln_in_b, w_in, pool_w, pool_scale, w_pool_proj, conv_w, w_conv_out, w_o, ln1_g, ln1_b, w_router, w_gate, w_up, w_down, ln2_g, ln2_b):
    assert w_in.shape == (1, D_MODEL, IN_COLS)
    wr = jnp.pad(w_router[0], ((0, 0), (0, LANES - N_EXPERTS)))
    wr_hi = wr.astype(BF16)
    wr_lo = (wr - wr_hi.astype(F32)).astype(BF16)
    p = {
        "ln_in_g": ln_in_g.reshape(1, D_MODEL), "ln_in_b": ln_in_b.reshape(1, D_MODEL),
        "w_in": w_in[0].astype(BF16), "pool_w": pool_w[0].astype(BF16), "pool_scale": pool_scale,
        "w_pool_proj": w_pool_proj[0].astype(BF16), "conv_w": conv_w[0],
        "w_conv_out": w_conv_out[0].astype(BF16), "w_o": w_o[0].astype(BF16),
        "ln1_g": ln1_g, "ln1_b": ln1_b,
        "wr_cat": jnp.concatenate([wr_hi, wr_lo], axis=1), "wr_hi": wr_hi,
        "w_gate": w_gate[0], "w_up": w_up[0], "w_down": w_down[0],
        "ln2_g": ln2_g, "ln2_b": ln2_b,
    }
    return _trunk(x_prompt, p), _trunk(x_sample, p)
```

```python
import functools

import jax
import jax.numpy as jnp
from jax import lax
from jax.experimental import pallas as pl
from jax.experimental.pallas import tpu as pltpu

F32 = jnp.float32
BF16 = jnp.bfloat16

D_MODEL = 1024
N_EXPERTS = 16
EC_CAPACITY_FACTOR = 2
POOL_WINDOWS = (2, 4, 8, 16)
POOL_GROUP = D_MODEL // len(POOL_WINDOWS)
ALPHA = 2.0 ** 0.25
LN_EPS = 1e-5
IN_COLS = 6 * D_MODEL

LANES = 128
SUBLANES = 8
BF16_ROWS = 16
MXU_COLS = 256
TM = 512
HALO = SUBLANES
TB = 512
CBW = 128
FFN_ROWS = 512
GATE_COLS = LANES
ROW_COLS = D_MODEL + GATE_COLS
VMEM_LIMIT = 56 * 1024 * 1024


def _layer_norm(x, g, b):
    mu = jnp.mean(x, axis=-1, keepdims=True)
    xc = x - mu
    var = jnp.mean(xc * xc, axis=-1, keepdims=True)
    return xc * lax.rsqrt(var + LN_EPS) * g + b


def _dot(a, b):
    return jnp.dot(a, b, preferred_element_type=F32)


def _onehot(mask):
    return jnp.where(mask, 1.0, 0.0).astype(BF16)


def _mixer_kernel(seq_len, x_ref, xprev_ref, xnext_ref, lng_ref, lnb_ref, win_ref, poolw_ref,
                  pscale_ref, wpp_ref, convw_ref, wco_ref, wo_ref, ln1g_ref, ln1b_ref,
                  wrcat_ref, wrhi_ref, x1_ref, xext_ref, afft_ref, resid):
    i = pl.program_id(0)
    n_tiles = pl.num_programs(0) - 1

    def finish_previous():
        x1 = _layer_norm(resid[...], ln1g_ref[...], ln1b_ref[...])
        x1_ref[...] = x1

        x1h = x1.astype(BF16)
        x1l = (x1 - x1h.astype(F32)).astype(BF16)
        lg = _dot(x1h, wrcat_ref[...])
        logits = lg[:, :LANES] + lg[:, LANES:] + _dot(x1l, wrhi_ref[...])
        lane = lax.broadcasted_iota(jnp.int32, (TM, LANES), 1)
        logits = jnp.where(lane < N_EXPERTS, logits, -1e30)
        ex = jnp.exp(logits - jnp.max(logits, axis=1, keepdims=True))
        aff = ex / jnp.sum(ex, axis=1, keepdims=True)
        afft_ref[...] = aff.T[:N_EXPERTS, :]

        a1 = aff.astype(BF16).astype(F32)
        r1 = aff - a1
        a2 = r1.astype(BF16).astype(F32)
        a3 = (r1 - a2).astype(BF16).astype(F32)
        ext = a1 + pltpu.roll(a2, N_EXPERTS, 1) + pltpu.roll(a3, 2 * N_EXPERTS, 1)
        xext_ref[:, :D_MODEL] = x1h
        xext_ref[:, D_MODEL:] = ext.astype(BF16)

    @pl.when(i == 0)
    def _():
        resid[...] = jnp.zeros_like(resid)

    @pl.when(i < n_tiles)
    def _():
        _mixer_tile(seq_len, i, x_ref, xprev_ref, xnext_ref, lng_ref, lnb_ref, win_ref, poolw_ref,
                    pscale_ref, wpp_ref, convw_ref, wco_ref, wo_ref, resid, finish_previous)

    @pl.when(i == n_tiles)
    def _():
        finish_previous()


def _mixer_tile(seq_len, i, x_ref, xprev_ref, xnext_ref, lng_ref, lnb_ref, win_ref, poolw_ref,
                pscale_ref, wpp_ref, convw_ref, wco_ref, wo_ref, resid, finish_previous):
    p0 = lax.rem(i * TM, seq_len)
    rows = TM + 2 * HALO
    main = slice(HALO, HALO + TM)

    xa = jnp.concatenate([xprev_ref[...], x_ref[...], xnext_ref[...]], axis=0)
    xna = _layer_norm(xa, lng_ref[...], lnb_ref[...])
    xb = xna.astype(BF16)
    xn = xna[main]
    xb_main = xn.astype(BF16)

    ridx = lax.broadcasted_iota(jnp.int32, (rows, 1), 0)
    lo_keep = jnp.where(p0 == 0, HALO, 0)
    hi_keep = jnp.where(p0 + TM == seq_len, HALO + TM, rows)
    keep = jnp.logical_and(ridx >= lo_keep, ridx < hi_keep)

    assert POOL_GROUP == MXU_COLS

    def proj(lhs, k, q):
        c0 = k * D_MODEL + q * MXU_COLS
        return _dot(lhs, win_ref[:, c0:c0 + MXU_COLS])

    def later(v, k):
        return pltpu.roll(v, (rows - k) % rows, 0)

    pos = p0 + lax.broadcasted_iota(jnp.int32, (TM, 1), 0)

    def pooled(u, w):
        s, span = u, 1
        while span < w // 2:
            s = s + later(s, span)
            span *= 2
        s = later(s, -span) + s
        cnt = (jnp.minimum(pos + (w - w // 2), seq_len) - jnp.maximum(pos - w // 2, 0)).astype(F32)
        return (s[main] / cnt - u[main]).astype(BF16)

    cw = convw_ref[...]
    mixed, gated = [], []
    for q in range(D_MODEL // MXU_COLS):
        cols = slice(q * MXU_COLS, (q + 1) * MXU_COLS)
        u = jnp.where(keep, proj(xb, 0, q), 0.0)
        h = proj(xb, 1, q)
        c = proj(xb, 3, q)
        gate_b = proj(xb_main, 2, q)
        mixed.append((_dot(pooled(u, POOL_WINDOWS[q]), poolw_ref[q]) * pscale_ref[:, cols]).astype(BF16))
        z = jnp.where(keep, c * h, 0.0)
        conv = (later(z, -1) * cw[0:1, cols] + z * cw[1:2, cols] + later(z, 1) * cw[2:3, cols])[main]
        gated.append((gate_b * conv).astype(BF16))
    mixed = jnp.concatenate(mixed, axis=1)
    gated = jnp.concatenate(gated, axis=1)
    merged = []
    for q in range(D_MODEL // MXU_COLS):
        cols = slice(q * MXU_COLS, (q + 1) * MXU_COLS)
        y_pool = _dot(mixed, wpp_ref[:, cols])
        y_conv = _dot(gated, wco_ref[:, cols])
        g_pool = proj(xb_main, 4, q)
        g_conv = proj(xb_main, 5, q)
        merged.append((jax.nn.sigmoid(g_pool) * y_pool + jax.nn.sigmoid(g_conv) * y_conv).astype(BF16))
        if q == 0:
            finish_previous()
    m = _dot(jnp.concatenate(merged, axis=1), wo_ref[...])
    resid[...] = ALPHA * xn + m


def _mixer(x, seq_len, p):
    n_tok = x.shape[0]
    n_tiles = n_tok // TM
    assert n_tok % TM == 0 and seq_len % TM == 0
    tiles_per_halo = TM // HALO
    last_halo = n_tok // HALO - 1

    def const(shape):
        return pl.BlockSpec(shape, lambda i: (0,) * len(shape), pipeline_mode=pl.Buffered(1))

    def tile(i):
        return jnp.minimum(i, n_tiles - 1)

    def done(i):
        return jnp.maximum(i - 1, 0)

    in_specs = [
        pl.BlockSpec((TM, D_MODEL), lambda i: (tile(i), 0)),
        pl.BlockSpec((HALO, D_MODEL), lambda i: (jnp.maximum(tile(i) * tiles_per_halo - 1, 0), 0)),
        pl.BlockSpec((HALO, D_MODEL), lambda i: (jnp.minimum((tile(i) + 1) * tiles_per_halo, last_halo), 0)),
        const((1, D_MODEL)), const((1, D_MODEL)),
        const((D_MODEL, IN_COLS)),
        const((len(POOL_WINDOWS), POOL_GROUP, POOL_GROUP)),
        const((1, D_MODEL)),
        const((D_MODEL, D_MODEL)),
        const((3, D_MODEL)),
        const((D_MODEL, D_MODEL)),
        const((D_MODEL, D_MODEL)),
        const((1, D_MODEL)), const((1, D_MODEL)),
        const((D_MODEL, 2 * LANES)), const((D_MODEL, LANES)),
    ]
    out_specs = [
        pl.BlockSpec((TM, D_MODEL), lambda i: (done(i), 0)),
        pl.BlockSpec((TM, ROW_COLS), lambda i: (done(i), 0)),
        pl.BlockSpec((N_EXPERTS, TM), lambda i: (0, done(i))),
    ]
    out_shape = [
        jax.ShapeDtypeStruct((n_tok, D_MODEL), F32),
        jax.ShapeDtypeStruct((n_tok, D_MODEL + GATE_COLS), BF16),
        jax.ShapeDtypeStruct((N_EXPERTS, n_tok), F32),
    ]
    return pl.pallas_call(
        functools.partial(_mixer_kernel, seq_len),
        grid=(n_tiles + 1,),
        in_specs=in_specs,
        out_specs=out_specs,
        out_shape=out_shape,
        scratch_shapes=[pltpu.VMEM((TM, D_MODEL), F32)],
        compiler_params=pltpu.CompilerParams(dimension_semantics=("arbitrary",),
                                             vmem_limit_bytes=VMEM_LIMIT),
        name="mixer",
    )(x, x, x, p["ln_in_g"], p["ln_in_b"], p["w_in"], p["pool_w"], p["pool_scale"], p["w_pool_proj"],
      p["conv_w"], p["w_conv_out"], p["w_o"], p["ln1_g"], p["ln1_b"], p["wr_cat"], p["wr_hi"])


def _plan_kernel(cap, n_tiles, afft_ref, pos_ref, post_ref, starts_ref, sel_ref, eq_ref):
    n_tok = afft_ref.shape[1]
    bits = pltpu.bitcast(afft_ref[...], jnp.int32)

    def count(mask):
        return jnp.sum(jnp.where(mask, 1.0, 0.0), axis=1, keepdims=True)

    def bit_step(it, thr):
        cand = jnp.bitwise_or(thr, jnp.left_shift(jnp.int32(1), 30 - it))
        return jnp.where(count(bits >= cand) >= cap, cand, thr)

    thr = lax.fori_loop(0, 31, bit_step, jnp.zeros((N_EXPERTS, 1), jnp.int32))
    gt = bits > thr
    eq = bits == thr
    need = cap - count(gt)
    excess = jnp.max(count(eq) - need)
    sel_ref[...] = jnp.where(gt, 1.0, 0.0)
    eq_ref[...] = jnp.where(eq, 1.0, 0.0)
    r = lax.broadcasted_iota(jnp.int32, (LANES, LANES), 0)
    c = lax.broadcasted_iota(jnp.int32, (LANES, LANES), 1)

    @pl.when(excess <= 0.0)
    def _():
        sel_ref[...] = sel_ref[...] + eq_ref[...]

    @pl.when(excess > 0.0)
    def _():
        tri = _onehot(r <= c)

        def chunk(ci, seen):
            off = pl.multiple_of(ci * LANES, LANES)
            e_blk = eq_ref[:, pl.ds(off, LANES)]
            incl = _dot(e_blk.astype(BF16), tri)
            rank = seen + incl - e_blk
            sel_ref[:, pl.ds(off, LANES)] = sel_ref[:, pl.ds(off, LANES)] + jnp.where(rank < need, e_blk, 0.0)
            return seen + incl[:, LANES - 1:LANES]

        lax.fori_loop(0, n_tok // LANES, chunk, jnp.zeros((N_EXPERTS, 1), F32))

    lane = lax.broadcasted_iota(jnp.int32, (N_EXPERTS, LANES), 1)
    before = _onehot(r < c)
    unselected = jnp.full((LANES - N_EXPERTS, TB), -1.0, F32)
    run = jnp.zeros((N_EXPERTS, 1), F32)
    starts = jnp.zeros((N_EXPERTS, LANES), F32)
    for j in range(n_tiles):
        starts = jnp.where(lane == j, run, starts)
        base = run - BF16_ROWS * jnp.floor(run * (1.0 / BF16_ROWS))
        tile_pos = []
        for c0 in range(j * TB, (j + 1) * TB, LANES):
            blk = sel_ref[:, c0:c0 + LANES]
            tile_pos.append(jnp.where(blk > 0.5, _dot(blk.astype(BF16), before) + base, -1.0))
            total = jnp.sum(blk, axis=1, keepdims=True)
            base = base + total
            run = run + total
        tile_pos = jnp.concatenate(tile_pos, axis=1)
        pos_ref[:, j * TB:(j + 1) * TB] = tile_pos
        post_ref[j * TB:(j + 1) * TB, :] = jnp.concatenate([tile_pos, unselected], axis=0).T
    starts_ref[...] = starts.astype(jnp.int32)


def _plan(afft, cap):
    n_tok = afft.shape[1]
    n_tiles = n_tok // TB
    assert n_tiles <= LANES
    return pl.pallas_call(
        functools.partial(_plan_kernel, cap, n_tiles),
        out_shape=[jax.ShapeDtypeStruct((N_EXPERTS, n_tok), F32),
                   jax.ShapeDtypeStruct((n_tok, LANES), F32),
                   jax.ShapeDtypeStruct((N_EXPERTS, LANES), jnp.int32)],
        scratch_shapes=[pltpu.VMEM((N_EXPERTS, n_tok), F32), pltpu.VMEM((N_EXPERTS, n_tok), F32)],
        compiler_params=pltpu.CompilerParams(vmem_limit_bytes=VMEM_LIMIT),
        name="plan",
    )(afft)


def _expert_scalars(starts_ref, j, e):
    st = starts_ref[j, e]
    a = jnp.bitwise_and(st, -BF16_ROWS)
    return a, st - a, starts_ref[j + 1, e] - st


def _tile_scalars(starts_ref, j):
    return [_expert_scalars(starts_ref, j, e) for e in range(N_EXPERTS)]


def _passes(off, n):
    return lax.div(off + n + (CBW - 1), CBW)


def _max_passes(scal):
    last = scal[0][1] + scal[0][2]
    for _, off, n in scal[1:]:
        last = jnp.maximum(last, off + n)
    return _passes(last, 0)


def _window(a, k):
    return pl.ds(pl.multiple_of(a + k * CBW, BF16_ROWS), CBW)


def _gather_onehot(pos):
    row = lax.broadcasted_iota(jnp.int32, (CBW, 1), 0).astype(F32)
    return jnp.concatenate([_onehot(pos[e:e + 1, :] == row) for e in range(N_EXPERTS)], axis=0)


def _dispatch_kernel(starts_ref, pos_ref, pos_next_ref, xext_ref, xe_hbm,
                     wx0, wx1, hot0, hot1, ox, carry, sem, osem):
    j = pl.program_id(0)
    last_step = pl.num_programs(0) - 1
    slot = lax.rem(j, 2)
    wxs, hots = (wx0, wx1), (hot0, hot1)
    row = lax.broadcasted_iota(jnp.int32, (CBW, 1), 0).astype(F32)

    @pl.when(j == 0)
    def _():
        carry[...] = jnp.zeros_like(carry)
        for buf in (wx0, wx1, ox):
            buf[...] = jnp.zeros_like(buf)
        hot0[...] = _gather_onehot(pos_ref[...])

    def copy(s, e, a):
        return pltpu.make_async_copy(wxs[s].at[e, pl.ds(0, CBW)], xe_hbm.at[e, _window(a, 0)], sem.at[s])

    def partial_group_offset(a, off, n, k):
        return jnp.bitwise_and(a + off + n, -BF16_ROWS) - a - k * CBW

    scal = _tile_scalars(starts_ref, j)

    def first_windows(cur, nxt):
        wx = wxs[cur]
        n_chunks = D_MODEL // MXU_COLS
        per_chunk = N_EXPERTS // n_chunks
        for q in range(n_chunks + 1):
            cols = slice(q * MXU_COLS, min((q + 1) * MXU_COLS, ROW_COLS))
            w = _dot(hots[cur][...], xext_ref[:, cols])
            for e in range(N_EXPERTS):
                wx[e, 0:CBW, cols] = w[e * CBW:(e + 1) * CBW].astype(BF16)
            for e in range(q * per_chunk, min((q + 1) * per_chunk, N_EXPERTS)):
                hots[nxt][e * CBW:(e + 1) * CBW, :] = _onehot(pos_next_ref[e:e + 1, :] == row)
        head = slice(0, BF16_ROWS)
        for e, (a, off, n) in enumerate(scal):
            wx[e, head, :] = (wx[e, head, :].astype(F32) + carry[e].astype(F32)).astype(BF16)
            d = pl.multiple_of(jnp.minimum(partial_group_offset(a, off, n, 0), CBW), BF16_ROWS)
            carry[e] = wx[e, pl.ds(d, BF16_ROWS), :]

        @pl.when(j > 0)
        def _():
            for e in range(N_EXPERTS):
                copy(nxt, e, _expert_scalars(starts_ref, j - 1, e)[0]).wait()

        for e, (a, _, _) in enumerate(scal):
            copy(cur, e, a).start()

        @pl.when(j == last_step)
        def _():
            for e, (a, _, _) in enumerate(scal):
                copy(cur, e, a).wait()

    pl.when(slot == 0)(functools.partial(first_windows, 0, 1))
    pl.when(slot == 1)(functools.partial(first_windows, 1, 0))

    @pl.when(_max_passes(scal) > 1)
    def _():
        def expert(e, unused_e):
            a, off, n = _expert_scalars(starts_ref, j, e)
            prow = pos_ref[pl.ds(e, 1), :]

            def extra(k, unused):
                ox[0:CBW, :] = _dot(_onehot(prow == row + (k * CBW).astype(F32)), xext_ref[...]).astype(BF16)
                cp = pltpu.make_async_copy(ox.at[pl.ds(0, CBW)], xe_hbm.at[e, _window(a, k)], osem.at[0])
                cp.start()

                @pl.when(off + n <= (k + 1) * CBW)
                def _():
                    d = pl.multiple_of(partial_group_offset(a, off, n, k), BF16_ROWS)
                    carry[e] = ox[pl.ds(d, BF16_ROWS), :]

                cp.wait()
                return unused

            return lax.fori_loop(1, _passes(off, n), extra, unused_e)

        lax.fori_loop(0, N_EXPERTS, expert, 0)

    @pl.when(j == last_step)
    def _():
        ox[0:CBW, :] = jnp.zeros((CBW, ROW_COLS), BF16)
        tail = pl.ds(xe_hbm.shape[1] - CBW, CBW)
        pads = [pltpu.make_async_copy(ox.at[pl.ds(0, CBW)], xe_hbm.at[e, tail], osem.at[0])
                for e in range(N_EXPERTS)]
        for cp in pads:
            cp.start()
        for cp in pads:
            cp.wait()


def _dispatch(starts_tab, pos, xext, cap):
    n_tok = pos.shape[1]
    n_tiles = n_tok // TB
    rows = cap + CBW
    grid_spec = pltpu.PrefetchScalarGridSpec(
        num_scalar_prefetch=1,
        grid=(n_tiles,),
        in_specs=[pl.BlockSpec((N_EXPERTS, TB), lambda j, s: (0, j)),
                  pl.BlockSpec((N_EXPERTS, TB), lambda j, s: (0, jnp.minimum(j + 1, n_tiles - 1))),
                  pl.BlockSpec((TB, D_MODEL + GATE_COLS), lambda j, s: (j, 0))],
        out_specs=pl.BlockSpec(memory_space=pl.ANY),
        scratch_shapes=[pltpu.VMEM((N_EXPERTS, CBW + BF16_ROWS, ROW_COLS), BF16),
                        pltpu.VMEM((N_EXPERTS, CBW + BF16_ROWS, ROW_COLS), BF16),
                        pltpu.VMEM((N_EXPERTS * CBW, TB), BF16),
                        pltpu.VMEM((N_EXPERTS * CBW, TB), BF16),
                        pltpu.VMEM((CBW + BF16_ROWS, ROW_COLS), BF16),
                        pltpu.VMEM((N_EXPERTS, BF16_ROWS, ROW_COLS), BF16),
                        pltpu.SemaphoreType.DMA((2,)),
                        pltpu.SemaphoreType.DMA((1,))],
    )
    return pl.pallas_call(
        _dispatch_kernel,
        grid_spec=grid_spec,
        out_shape=jax.ShapeDtypeStruct((N_EXPERTS, rows, ROW_COLS), BF16),
        compiler_params=pltpu.CompilerParams(dimension_semantics=("arbitrary",),
                                             vmem_limit_bytes=VMEM_LIMIT),
        name="dispatch",
    )(starts_tab, pos, pos, xext)


def _ffn_kernel(cap, xe_ref, wg_ref, wu_ref, wd_ref, ye_ref, wg_b, wu_b, wd_b):
    e = pl.program_id(0)
    wg_b[...] = wg_ref[0].astype(BF16)
    wu_b[...] = wu_ref[0].astype(BF16)
    wd_b[...] = wd_ref[0].astype(BF16)
    lane = lax.broadcasted_iota(jnp.int32, (FFN_ROWS, GATE_COLS), 1)
    mine = jnp.logical_and(jnp.bitwise_and(lane, N_EXPERTS - 1) == e, lane < 3 * N_EXPERTS)
    for c0 in range(0, cap, FFN_ROWS):
        rows = slice(c0, c0 + FFN_ROWS)
        x = xe_ref[0, rows, :D_MODEL]
        act = jax.nn.silu(_dot(x, wg_b[...])) * _dot(x, wu_b[...])
        y = _dot(act.astype(BF16), wd_b[...])
        pieces = xe_ref[0, rows, D_MODEL:].astype(F32)
        gate = jnp.sum(jnp.where(mine, pieces, 0.0), axis=1, keepdims=True)
        ye_ref[0, rows, :] = (y * gate).astype(BF16)
    ye_ref[0, cap:, :] = jnp.zeros((ye_ref.shape[1] - cap, D_MODEL), BF16)


def _ffn(xe, w_gate, w_up, w_down, cap):
    rows = xe.shape[1]
    assert cap % FFN_ROWS == 0
    wspec = pl.BlockSpec((1, D_MODEL, D_MODEL), lambda e: (e, 0, 0))
    return pl.pallas_call(
        functools.partial(_ffn_kernel, cap),
        grid=(N_EXPERTS,),
        in_specs=[pl.BlockSpec((1, rows, ROW_COLS), lambda e: (e, 0, 0)),
                  wspec, wspec, wspec],
        out_specs=pl.BlockSpec((1, rows, D_MODEL), lambda e: (e, 0, 0)),
        out_shape=jax.ShapeDtypeStruct((N_EXPERTS, rows, D_MODEL), BF16),
        scratch_shapes=[pltpu.VMEM((D_MODEL, D_MODEL), BF16)] * 3,
        compiler_params=pltpu.CompilerParams(dimension_semantics=("arbitrary",),
                                             vmem_limit_bytes=VMEM_LIMIT),
        name="ffn",
    )(xe, w_gate, w_up, w_down)


def _scatter_onehot(pos):
    col = lax.broadcasted_iota(jnp.int32, (1, CBW), 1).astype(F32)
    return jnp.concatenate([_onehot(pos[:, e:e + 1] == col) for e in range(N_EXPERTS)], axis=1)


def _combine_kernel(starts_ref, post_ref, post_next_ref, x1_ref, ln2g_ref, ln2b_ref, ye_hbm, out_ref,
                    ybuf0, ybuf1, hot0, hot1, acc0, acc1, obuf, sem, osem):
    j = pl.program_id(0)
    n_tiles = pl.num_programs(0) - 1
    slot = lax.rem(j, 2)
    ybufs, hots, accs = (ybuf0, ybuf1), (hot0, hot1), (acc0, acc1)

    def fetch(s, tile):
        return [pltpu.make_async_copy(ye_hbm.at[e, _window(_expert_scalars(starts_ref, tile, e)[0], 0)],
                                      ybufs[s].at[e], sem.at[s]) for e in range(N_EXPERTS)]

    @pl.when(j == 0)
    def _():
        for cp in fetch(0, 0):
            cp.start()
        hot0[...] = _scatter_onehot(post_ref[...])
        acc1[...] = jnp.zeros_like(acc1)

    def step(cur, nxt):
        @pl.when(j + 1 < n_tiles)
        def _():
            for cp in fetch(nxt, j + 1):
                cp.start()

        @pl.when(j < n_tiles)
        def _():
            for cp in fetch(cur, j):
                cp.wait()

        n_chunks = D_MODEL // MXU_COLS
        per_chunk = N_EXPERTS // n_chunks
        rows_chunk = TB // n_chunks
        col = lax.broadcasted_iota(jnp.int32, (1, CBW), 1).astype(F32)
        for q in range(n_chunks):
            cols = slice(q * MXU_COLS, (q + 1) * MXU_COLS)
            accs[cur][:, cols] = _dot(hots[cur][...],
                                      ybufs[cur][:, :, cols].reshape(N_EXPERTS * CBW, MXU_COLS))
            for e in range(q * per_chunk, (q + 1) * per_chunk):
                hots[nxt][:, e * CBW:(e + 1) * CBW] = _onehot(post_next_ref[:, e:e + 1] == col)
            rows = slice(q * rows_chunk, (q + 1) * rows_chunk)
            out_ref[rows, :] = _layer_norm(ALPHA * x1_ref[rows, :] + accs[nxt][rows, :],
                                           ln2g_ref[...], ln2b_ref[...])

        tile = jnp.minimum(j, n_tiles - 1)

        @pl.when(jnp.logical_and(j < n_tiles, _max_passes(_tile_scalars(starts_ref, tile)) > 1))
        def _():
            lane = lax.broadcasted_iota(jnp.int32, (1, LANES), 1)
            col = lax.broadcasted_iota(jnp.int32, (1, CBW), 1).astype(F32)

            def expert(e, carry):
                a, off, n = _expert_scalars(starts_ref, tile, e)
                pcol = jnp.sum(jnp.where(lane == e, post_ref[...], 0.0), axis=1, keepdims=True)

                def extra(k, carry2):
                    cp = pltpu.make_async_copy(ye_hbm.at[e, _window(a, k)], obuf, osem.at[0])
                    cp.start()
                    cp.wait()
                    accs[cur][...] += _dot(_onehot(pcol == col + (k * CBW).astype(F32)), obuf[...])
                    return carry2

                return lax.fori_loop(1, _passes(off, n), extra, carry)

            lax.fori_loop(0, N_EXPERTS, expert, 0)

    pl.when(slot == 0)(functools.partial(step, 0, 1))
    pl.when(slot == 1)(functools.partial(step, 1, 0))


def _combine(starts_tab, post, x1, ye, ln2_g, ln2_b):
    n_tok = post.shape[0]
    n_tiles = n_tok // TB
    assert n_tiles >= 2
    last = n_tiles - 1
    grid_spec = pltpu.PrefetchScalarGridSpec(
        num_scalar_prefetch=1,
        grid=(n_tiles + 1,),
        in_specs=[pl.BlockSpec((TB, LANES), lambda j, s: (jnp.minimum(j, last), 0)),
                  pl.BlockSpec((TB, LANES), lambda j, s: (jnp.minimum(j + 1, last), 0)),
                  pl.BlockSpec((TB, D_MODEL), lambda j, s: (jnp.maximum(j - 1, 0), 0)),
                  pl.BlockSpec((1, D_MODEL), lambda j, s: (0, 0)),
                  pl.BlockSpec((1, D_MODEL), lambda j, s: (0, 0)),
                  pl.BlockSpec(memory_space=pl.ANY)],
        out_specs=pl.BlockSpec((TB, D_MODEL), lambda j, s: (jnp.maximum(j - 1, 0), 0)),
        scratch_shapes=[pltpu.VMEM((N_EXPERTS, CBW, D_MODEL), BF16),
                        pltpu.VMEM((N_EXPERTS, CBW, D_MODEL), BF16),
                        pltpu.VMEM((TB, N_EXPERTS * CBW), BF16),
                        pltpu.VMEM((TB, N_EXPERTS * CBW), BF16),
                        pltpu.VMEM((TB, D_MODEL), F32),
                        pltpu.VMEM((TB, D_MODEL), F32),
                        pltpu.VMEM((CBW, D_MODEL), BF16),
                        pltpu.SemaphoreType.DMA((2,)),
                        pltpu.SemaphoreType.DMA((1,))],
    )
    return pl.pallas_call(
        _combine_kernel,
        grid_spec=grid_spec,
        out_shape=jax.ShapeDtypeStruct((n_tok, D_MODEL), F32),
        compiler_params=pltpu.CompilerParams(dimension_semantics=("arbitrary",),
                                             vmem_limit_bytes=VMEM_LIMIT),
        name="combine",
    )(starts_tab, post, post, x1, ln2_g, ln2_b, ye)


def _trunk(x, p):
    b, s, d = x.shape
    n_tok = b * s
    cap = EC_CAPACITY_FACTOR * n_tok // N_EXPERTS
    n_tiles = n_tok // TB
    x1, xext, afft = _mixer(x.reshape(n_tok, d), s, p)
    pos, post, starts = _plan(afft, cap)
    starts_tab = jnp.concatenate([starts[:, :n_tiles].T, jnp.full((1, N_EXPERTS), cap, jnp.int32)], axis=0)
    xe = _dispatch(starts_tab, pos, xext, cap)
    ye = _ffn(xe, p["w_gate"], p["w_up"], p["w_down"], cap)
    y = _combine(starts_tab, post, x1, ye, p["ln2_g"], p["ln2_b"])
    return y.reshape(b, s, d)


def kernel(x_prompt, x_sample, ln_in_g, ln_in_b, w_in, pool_w, pool_scale, w_pool_proj, conv_w, w_conv_out, w_o, ln1_g, ln1_b, w_router, w_gate, w_up, w_down, ln2_g, ln2_b):
    assert w_in.shape == (1, D_MODEL, IN_COLS)
    wr = jnp.pad(w_router[0], ((0, 0), (0, LANES - N_EXPERTS)))
    wr_hi = wr.astype(BF16)
    wr_lo = (wr - wr_hi.astype(F32)).astype(BF16)
    p = {
        "ln_in_g": ln_in_g.reshape(1, D_MODEL), "ln_in_b": ln_in_b.reshape(1, D_MODEL),
        "w_in": w_in[0].astype(BF16), "pool_w": pool_w[0].astype(BF16), "pool_scale": pool_scale,
        "w_pool_proj": w_pool_proj[0].astype(BF16), "conv_w": conv_w[0],
        "w_conv_out": w_conv_out[0].astype(BF16), "w_o": w_o[0].astype(BF16),
        "ln1_g": ln1_g, "ln1_b": ln1_b,
        "wr_cat": jnp.concatenate([wr_hi, wr_lo], axis=1), "wr_hi": wr_hi,
        "w_gate": w_gate[0], "w_up": w_up[0], "w_down": w_down[0],
        "ln2_g": ln2_g, "ln2_b": ln2_b,
    }
    return _trunk(x_prompt, p), _trunk(x_sample, p)
```

```python
import functools

import jax
import jax.numpy as jnp
from jax import lax
from jax.experimental import pallas as pl
from jax.experimental.pallas import tpu as pltpu

F32 = jnp.float32
BF16 = jnp.bfloat16

D_MODEL = 1024
N_EXPERTS = 16
EC_CAPACITY_FACTOR = 2
POOL_WINDOWS = (2, 4, 8, 16)
POOL_GROUP = D_MODEL // len(POOL_WINDOWS)
ALPHA = 2.0 ** 0.25
LN_EPS = 1e-5
IN_COLS = 6 * D_MODEL

LANES = 128
SUBLANES = 8
BF16_ROWS = 16
MXU_COLS = 256
TM = 512
HALO = SUBLANES
TB = 512
CBW = 96
SLOT = LANES
FFN_ROWS = 512
GATE_COLS = LANES
ROW_COLS = D_MODEL + GATE_COLS
VMEM_LIMIT = 56 * 1024 * 1024


def _layer_norm(x, g, b):
    mu = jnp.mean(x, axis=-1, keepdims=True)
    xc = x - mu
    var = jnp.mean(xc * xc, axis=-1, keepdims=True)
    return xc * lax.rsqrt(var + LN_EPS) * g + b


def _dot(a, b):
    return jnp.dot(a, b, preferred_element_type=F32)


def _onehot(mask):
    return jnp.where(mask, 1.0, 0.0).astype(BF16)


def _mixer_kernel(seq_len, x_ref, xprev_ref, xnext_ref, lng_ref, lnb_ref, win_ref, poolw_ref,
                  pscale_ref, wpp_ref, convw_ref, wco_ref, wo_ref, ln1g_ref, ln1b_ref,
                  wrcat_ref, wrhi_ref, x1_ref, xext_ref, afft_ref, resid):
    i = pl.program_id(0)
    n_tiles = pl.num_programs(0) - 1

    def finish_previous():
        x1 = _layer_norm(resid[...], ln1g_ref[...], ln1b_ref[...])
        x1_ref[...] = x1

        x1h = x1.astype(BF16)
        x1l = (x1 - x1h.astype(F32)).astype(BF16)
        lg = _dot(x1h, wrcat_ref[...])
        logits = lg[:, :LANES] + lg[:, LANES:] + _dot(x1l, wrhi_ref[...])
        lane = lax.broadcasted_iota(jnp.int32, (TM, LANES), 1)
        logits = jnp.where(lane < N_EXPERTS, logits, -1e30)
        ex = jnp.exp(logits - jnp.max(logits, axis=1, keepdims=True))
        aff = ex / jnp.sum(ex, axis=1, keepdims=True)
        afft_ref[...] = aff.T[:N_EXPERTS, :]

        a1 = aff.astype(BF16).astype(F32)
        r1 = aff - a1
        a2 = r1.astype(BF16).astype(F32)
        a3 = (r1 - a2).astype(BF16).astype(F32)
        ext = a1 + pltpu.roll(a2, N_EXPERTS, 1) + pltpu.roll(a3, 2 * N_EXPERTS, 1)
        xext_ref[:, :D_MODEL] = x1h
        xext_ref[:, D_MODEL:] = ext.astype(BF16)

    @pl.when(i == 0)
    def _():
        resid[...] = jnp.zeros_like(resid)

    @pl.when(i < n_tiles)
    def _():
        _mixer_tile(seq_len, i, x_ref, xprev_ref, xnext_ref, lng_ref, lnb_ref, win_ref, poolw_ref,
                    pscale_ref, wpp_ref, convw_ref, wco_ref, wo_ref, resid, finish_previous)

    @pl.when(i == n_tiles)
    def _():
        finish_previous()


def _mixer_tile(seq_len, i, x_ref, xprev_ref, xnext_ref, lng_ref, lnb_ref, win_ref, poolw_ref,
                pscale_ref, wpp_ref, convw_ref, wco_ref, wo_ref, resid, finish_previous):
    p0 = lax.rem(i * TM, seq_len)
    rows = TM + 2 * HALO
    main = slice(HALO, HALO + TM)

    xa = jnp.concatenate([xprev_ref[...], x_ref[...], xnext_ref[...]], axis=0)
    xna = _layer_norm(xa, lng_ref[...], lnb_ref[...])
    xb = xna.astype(BF16)
    xn = xna[main]
    xb_main = xn.astype(BF16)

    ridx = lax.broadcasted_iota(jnp.int32, (rows, 1), 0)
    lo_keep = jnp.where(p0 == 0, HALO, 0)
    hi_keep = jnp.where(p0 + TM == seq_len, HALO + TM, rows)
    keep = jnp.logical_and(ridx >= lo_keep, ridx < hi_keep)

    assert POOL_GROUP == MXU_COLS

    def proj(lhs, k, q):
        c0 = k * D_MODEL + q * MXU_COLS
        return _dot(lhs, win_ref[:, c0:c0 + MXU_COLS])

    def later(v, k):
        return pltpu.roll(v, (rows - k) % rows, 0)

    pos = p0 + lax.broadcasted_iota(jnp.int32, (TM, 1), 0)

    def pooled(u, w):
        s, span = u, 1
        while span < w // 2:
            s = s + later(s, span)
            span *= 2
        s = later(s, -span) + s
        cnt = (jnp.minimum(pos + (w - w // 2), seq_len) - jnp.maximum(pos - w // 2, 0)).astype(F32)
        return (s[main] / cnt - u[main]).astype(BF16)

    cw = convw_ref[...]
    mixed, gated = [], []
    for q in range(D_MODEL // MXU_COLS):
        cols = slice(q * MXU_COLS, (q + 1) * MXU_COLS)
        u = jnp.where(keep, proj(xb, 0, q), 0.0)
        h = proj(xb, 1, q)
        c = proj(xb, 3, q)
        gate_b = proj(xb_main, 2, q)
        mixed.append((_dot(pooled(u, POOL_WINDOWS[q]), poolw_ref[q]) * pscale_ref[:, cols]).astype(BF16))
        z = jnp.where(keep, c * h, 0.0)
        conv = (later(z, -1) * cw[0:1, cols] + z * cw[1:2, cols] + later(z, 1) * cw[2:3, cols])[main]
        gated.append((gate_b * conv).astype(BF16))
    mixed = jnp.concatenate(mixed, axis=1)
    gated = jnp.concatenate(gated, axis=1)
    merged = []
    for q in range(D_MODEL // MXU_COLS):
        cols = slice(q * MXU_COLS, (q + 1) * MXU_COLS)
        y_pool = _dot(mixed, wpp_ref[:, cols])
        y_conv = _dot(gated, wco_ref[:, cols])
        g_pool = proj(xb_main, 4, q)
        g_conv = proj(xb_main, 5, q)
        merged.append((jax.nn.sigmoid(g_pool) * y_pool + jax.nn.sigmoid(g_conv) * y_conv).astype(BF16))
        if q == 0:
            finish_previous()
    m = _dot(jnp.concatenate(merged, axis=1), wo_ref[...])
    resid[...] = ALPHA * xn + m


def _mixer(x, seq_len, p):
    n_tok = x.shape[0]
    n_tiles = n_tok // TM
    assert n_tok % TM == 0 and seq_len % TM == 0
    tiles_per_halo = TM // HALO
    last_halo = n_tok // HALO - 1

    def const(shape):
        return pl.BlockSpec(shape, lambda i: (0,) * len(shape), pipeline_mode=pl.Buffered(1))

    def tile(i):
        return jnp.minimum(i, n_tiles - 1)

    def done(i):
        return jnp.maximum(i - 1, 0)

    in_specs = [
        pl.BlockSpec((TM, D_MODEL), lambda i: (tile(i), 0)),
        pl.BlockSpec((HALO, D_MODEL), lambda i: (jnp.maximum(tile(i) * tiles_per_halo - 1, 0), 0)),
        pl.BlockSpec((HALO, D_MODEL), lambda i: (jnp.minimum((tile(i) + 1) * tiles_per_halo, last_halo), 0)),
        const((1, D_MODEL)), const((1, D_MODEL)),
        const((D_MODEL, IN_COLS)),
        const((len(POOL_WINDOWS), POOL_GROUP, POOL_GROUP)),
        const((1, D_MODEL)),
        const((D_MODEL, D_MODEL)),
        const((3, D_MODEL)),
        const((D_MODEL, D_MODEL)),
        const((D_MODEL, D_MODEL)),
        const((1, D_MODEL)), const((1, D_MODEL)),
        const((D_MODEL, 2 * LANES)), const((D_MODEL, LANES)),
    ]
    out_specs = [
        pl.BlockSpec((TM, D_MODEL), lambda i: (done(i), 0)),
        pl.BlockSpec((TM, ROW_COLS), lambda i: (done(i), 0)),
        pl.BlockSpec((N_EXPERTS, TM), lambda i: (0, done(i))),
    ]
    out_shape = [
        jax.ShapeDtypeStruct((n_tok, D_MODEL), F32),
        jax.ShapeDtypeStruct((n_tok, D_MODEL + GATE_COLS), BF16),
        jax.ShapeDtypeStruct((N_EXPERTS, n_tok), F32),
    ]
    return pl.pallas_call(
        functools.partial(_mixer_kernel, seq_len),
        grid=(n_tiles + 1,),
        in_specs=in_specs,
        out_specs=out_specs,
        out_shape=out_shape,
        scratch_shapes=[pltpu.VMEM((TM, D_MODEL), F32)],
        compiler_params=pltpu.CompilerParams(dimension_semantics=("arbitrary",),
                                             vmem_limit_bytes=VMEM_LIMIT),
        name="mixer",
    )(x, x, x, p["ln_in_g"], p["ln_in_b"], p["w_in"], p["pool_w"], p["pool_scale"], p["w_pool_proj"],
      p["conv_w"], p["w_conv_out"], p["w_o"], p["ln1_g"], p["ln1_b"], p["wr_cat"], p["wr_hi"])


def _plan_kernel(cap, n_tiles, afft_ref, pos_ref, post_ref, starts_ref, sel_ref, eq_ref):
    n_tok = afft_ref.shape[1]
    bits = pltpu.bitcast(afft_ref[...], jnp.int32)

    def count(mask):
        return jnp.sum(jnp.where(mask, 1.0, 0.0), axis=1, keepdims=True)

    def bit_step(it, thr):
        cand = jnp.bitwise_or(thr, jnp.left_shift(jnp.int32(1), 30 - it))
        return jnp.where(count(bits >= cand) >= cap, cand, thr)

    thr = lax.fori_loop(0, 31, bit_step, jnp.zeros((N_EXPERTS, 1), jnp.int32))
    gt = bits > thr
    eq = bits == thr
    need = cap - count(gt)
    excess = jnp.max(count(eq) - need)
    sel_ref[...] = jnp.where(gt, 1.0, 0.0)
    eq_ref[...] = jnp.where(eq, 1.0, 0.0)
    r = lax.broadcasted_iota(jnp.int32, (LANES, LANES), 0)
    c = lax.broadcasted_iota(jnp.int32, (LANES, LANES), 1)

    @pl.when(excess <= 0.0)
    def _():
        sel_ref[...] = sel_ref[...] + eq_ref[...]

    @pl.when(excess > 0.0)
    def _():
        tri = _onehot(r <= c)

        def chunk(ci, seen):
            off = pl.multiple_of(ci * LANES, LANES)
            e_blk = eq_ref[:, pl.ds(off, LANES)]
            incl = _dot(e_blk.astype(BF16), tri)
            rank = seen + incl - e_blk
            sel_ref[:, pl.ds(off, LANES)] = sel_ref[:, pl.ds(off, LANES)] + jnp.where(rank < need, e_blk, 0.0)
            return seen + incl[:, LANES - 1:LANES]

        lax.fori_loop(0, n_tok // LANES, chunk, jnp.zeros((N_EXPERTS, 1), F32))

    lane = lax.broadcasted_iota(jnp.int32, (N_EXPERTS, LANES), 1)
    before = _onehot(r < c)
    unselected = jnp.full((LANES - N_EXPERTS, TB), -1.0, F32)
    run = jnp.zeros((N_EXPERTS, 1), F32)
    starts = jnp.zeros((N_EXPERTS, LANES), F32)
    for j in range(n_tiles):
        starts = jnp.where(lane == j, run, starts)
        base = run - BF16_ROWS * jnp.floor(run * (1.0 / BF16_ROWS))
        tile_pos = []
        for c0 in range(j * TB, (j + 1) * TB, LANES):
            blk = sel_ref[:, c0:c0 + LANES]
            tile_pos.append(jnp.where(blk > 0.5, _dot(blk.astype(BF16), before) + base, -1.0))
            total = jnp.sum(blk, axis=1, keepdims=True)
            base = base + total
            run = run + total
        tile_pos = jnp.concatenate(tile_pos, axis=1)
        pos_ref[:, j * TB:(j + 1) * TB] = tile_pos
        post_ref[j * TB:(j + 1) * TB, :] = jnp.concatenate([tile_pos, unselected], axis=0).T
    starts_ref[...] = starts.astype(jnp.int32)


def _plan(afft, cap):
    n_tok = afft.shape[1]
    n_tiles = n_tok // TB
    assert n_tiles <= LANES
    return pl.pallas_call(
        functools.partial(_plan_kernel, cap, n_tiles),
        out_shape=[jax.ShapeDtypeStruct((N_EXPERTS, n_tok), F32),
                   jax.ShapeDtypeStruct((n_tok, LANES), F32),
                   jax.ShapeDtypeStruct((N_EXPERTS, LANES), jnp.int32)],
        scratch_shapes=[pltpu.VMEM((N_EXPERTS, n_tok), F32), pltpu.VMEM((N_EXPERTS, n_tok), F32)],
        compiler_params=pltpu.CompilerParams(vmem_limit_bytes=VMEM_LIMIT),
        name="plan",
    )(afft)


def _expert_scalars(starts_ref, j, e):
    st = starts_ref[j, e]
    a = jnp.bitwise_and(st, -BF16_ROWS)
    return a, st - a, starts_ref[j + 1, e] - st


def _tile_scalars(starts_ref, j):
    return [_expert_scalars(starts_ref, j, e) for e in range(N_EXPERTS)]


def _passes(off, n):
    return lax.div(off + n + (CBW - 1), CBW)


def _max_passes(scal):
    last = scal[0][1] + scal[0][2]
    for _, off, n in scal[1:]:
        last = jnp.maximum(last, off + n)
    return _passes(last, 0)


def _window(a, k):
    return pl.ds(pl.multiple_of(a + k * CBW, BF16_ROWS), CBW)


def _gather_onehot(pos):
    row = lax.broadcasted_iota(jnp.int32, (CBW, 1), 0).astype(F32)
    return jnp.concatenate([_onehot(pos[e:e + 1, :] == row) for e in range(N_EXPERTS)], axis=0)


def _dispatch_kernel(starts_ref, pos_ref, pos_next_ref, xext_ref, xe_hbm,
                     wx0, wx1, hot0, hot1, ox, carry, sem, osem):
    j = pl.program_id(0)
    last_step = pl.num_programs(0) - 1
    slot = lax.rem(j, 2)
    wxs, hots = (wx0, wx1), (hot0, hot1)
    row = lax.broadcasted_iota(jnp.int32, (CBW, 1), 0).astype(F32)

    @pl.when(j == 0)
    def _():
        carry[...] = jnp.zeros_like(carry)
        for buf in (wx0, wx1, ox):
            buf[...] = jnp.zeros_like(buf)
        hot0[...] = _gather_onehot(pos_ref[...])

    def copy(s, e, a):
        return pltpu.make_async_copy(wxs[s].at[e, pl.ds(0, CBW)], xe_hbm.at[e, _window(a, 0)], sem.at[s])

    def partial_group_offset(a, off, n, k):
        return jnp.bitwise_and(a + off + n, -BF16_ROWS) - a - k * CBW

    scal = _tile_scalars(starts_ref, j)

    def first_windows(cur, nxt):
        wx = wxs[cur]
        n_chunks = D_MODEL // MXU_COLS
        per_chunk = N_EXPERTS // n_chunks
        for q in range(n_chunks + 1):
            cols = slice(q * MXU_COLS, min((q + 1) * MXU_COLS, ROW_COLS))
            w = _dot(hots[cur][...], xext_ref[:, cols])
            for e in range(N_EXPERTS):
                wx[e, 0:CBW, cols] = w[e * CBW:(e + 1) * CBW].astype(BF16)
            for e in range(q * per_chunk, min((q + 1) * per_chunk, N_EXPERTS)):
                hots[nxt][e * CBW:(e + 1) * CBW, :] = _onehot(pos_next_ref[e:e + 1, :] == row)
        head = slice(0, BF16_ROWS)
        for e, (a, off, n) in enumerate(scal):
            wx[e, head, :] = (wx[e, head, :].astype(F32) + carry[e].astype(F32)).astype(BF16)
            d = pl.multiple_of(jnp.minimum(partial_group_offset(a, off, n, 0), CBW), BF16_ROWS)
            carry[e] = wx[e, pl.ds(d, BF16_ROWS), :]

        @pl.when(j > 0)
        def _():
            for e in range(N_EXPERTS):
                copy(nxt, e, _expert_scalars(starts_ref, j - 1, e)[0]).wait()

        for e, (a, _, _) in enumerate(scal):
            copy(cur, e, a).start()

        @pl.when(j == last_step)
        def _():
            for e, (a, _, _) in enumerate(scal):
                copy(cur, e, a).wait()

    pl.when(slot == 0)(functools.partial(first_windows, 0, 1))
    pl.when(slot == 1)(functools.partial(first_windows, 1, 0))

    @pl.when(_max_passes(scal) > 1)
    def _():
        def expert(e, unused_e):
            a, off, n = _expert_scalars(starts_ref, j, e)
            prow = pos_ref[pl.ds(e, 1), :]

            def extra(k, unused):
                ox[0:CBW, :] = _dot(_onehot(prow == row + (k * CBW).astype(F32)), xext_ref[...]).astype(BF16)
                cp = pltpu.make_async_copy(ox.at[pl.ds(0, CBW)], xe_hbm.at[e, _window(a, k)], osem.at[0])
                cp.start()

                @pl.when(off + n <= (k + 1) * CBW)
                def _():
                    d = pl.multiple_of(partial_group_offset(a, off, n, k), BF16_ROWS)
                    carry[e] = ox[pl.ds(d, BF16_ROWS), :]

                cp.wait()
                return unused

            return lax.fori_loop(1, _passes(off, n), extra, unused_e)

        lax.fori_loop(0, N_EXPERTS, expert, 0)

    @pl.when(j == last_step)
    def _():
        ox[0:CBW, :] = jnp.zeros((CBW, ROW_COLS), BF16)
        tail = pl.ds(xe_hbm.shape[1] - CBW, CBW)
        pads = [pltpu.make_async_copy(ox.at[pl.ds(0, CBW)], xe_hbm.at[e, tail], osem.at[0])
                for e in range(N_EXPERTS)]
        for cp in pads:
            cp.start()
        for cp in pads:
            cp.wait()


def _dispatch(starts_tab, pos, xext, cap):
    n_tok = pos.shape[1]
    n_tiles = n_tok // TB
    rows = cap + CBW
    grid_spec = pltpu.PrefetchScalarGridSpec(
        num_scalar_prefetch=1,
        grid=(n_tiles,),
        in_specs=[pl.BlockSpec((N_EXPERTS, TB), lambda j, s: (0, j)),
                  pl.BlockSpec((N_EXPERTS, TB), lambda j, s: (0, jnp.minimum(j + 1, n_tiles - 1))),
                  pl.BlockSpec((TB, D_MODEL + GATE_COLS), lambda j, s: (j, 0))],
        out_specs=pl.BlockSpec(memory_space=pl.ANY),
        scratch_shapes=[pltpu.VMEM((N_EXPERTS, CBW + BF16_ROWS, ROW_COLS), BF16),
                        pltpu.VMEM((N_EXPERTS, CBW + BF16_ROWS, ROW_COLS), BF16),
                        pltpu.VMEM((N_EXPERTS * CBW, TB), BF16),
                        pltpu.VMEM((N_EXPERTS * CBW, TB), BF16),
                        pltpu.VMEM((CBW + BF16_ROWS, ROW_COLS), BF16),
                        pltpu.VMEM((N_EXPERTS, BF16_ROWS, ROW_COLS), BF16),
                        pltpu.SemaphoreType.DMA((2,)),
                        pltpu.SemaphoreType.DMA((1,))],
    )
    return pl.pallas_call(
        _dispatch_kernel,
        grid_spec=grid_spec,
        out_shape=jax.ShapeDtypeStruct((N_EXPERTS, rows, ROW_COLS), BF16),
        compiler_params=pltpu.CompilerParams(dimension_semantics=("arbitrary",),
                                             vmem_limit_bytes=VMEM_LIMIT),
        name="dispatch",
    )(starts_tab, pos, pos, xext)


def _ffn_kernel(cap, xe_ref, wg_ref, wu_ref, wd_ref, ye_ref, wg_b, wu_b, wd_b):
    e = pl.program_id(0)
    wg_b[...] = wg_ref[0].astype(BF16)
    wu_b[...] = wu_ref[0].astype(BF16)
    wd_b[...] = wd_ref[0].astype(BF16)
    lane = lax.broadcasted_iota(jnp.int32, (FFN_ROWS, GATE_COLS), 1)
    mine = jnp.logical_and(jnp.bitwise_and(lane, N_EXPERTS - 1) == e, lane < 3 * N_EXPERTS)
    for c0 in range(0, cap, FFN_ROWS):
        rows = slice(c0, c0 + FFN_ROWS)
        x = xe_ref[0, rows, :D_MODEL]
        act = jax.nn.silu(_dot(x, wg_b[...])) * _dot(x, wu_b[...])
        y = _dot(act.astype(BF16), wd_b[...])
        pieces = xe_ref[0, rows, D_MODEL:].astype(F32)
        gate = jnp.sum(jnp.where(mine, pieces, 0.0), axis=1, keepdims=True)
        ye_ref[0, rows, :] = (y * gate).astype(BF16)
    ye_ref[0, cap:, :] = jnp.zeros((ye_ref.shape[1] - cap, D_MODEL), BF16)


def _ffn(xe, w_gate, w_up, w_down, cap):
    rows = xe.shape[1]
    assert cap % FFN_ROWS == 0
    wspec = pl.BlockSpec((1, D_MODEL, D_MODEL), lambda e: (e, 0, 0))
    return pl.pallas_call(
        functools.partial(_ffn_kernel, cap),
        grid=(N_EXPERTS,),
        in_specs=[pl.BlockSpec((1, rows, ROW_COLS), lambda e: (e, 0, 0)),
                  wspec, wspec, wspec],
        out_specs=pl.BlockSpec((1, rows, D_MODEL), lambda e: (e, 0, 0)),
        out_shape=jax.ShapeDtypeStruct((N_EXPERTS, rows, D_MODEL), BF16),
        scratch_shapes=[pltpu.VMEM((D_MODEL, D_MODEL), BF16)] * 3,
        compiler_params=pltpu.CompilerParams(dimension_semantics=("arbitrary",),
                                             vmem_limit_bytes=VMEM_LIMIT),
        name="ffn",
    )(xe, w_gate, w_up, w_down)


def _scatter_onehot(pos):
    col = lax.broadcasted_iota(jnp.int32, (1, SLOT), 1).astype(F32)
    return jnp.concatenate([_onehot(pos[:, e:e + 1] == col) for e in range(N_EXPERTS)], axis=1)


def _combine_kernel(starts_ref, post_ref, post_next_ref, x1_ref, ln2g_ref, ln2b_ref, ye_hbm, out_ref,
                    ybuf0, ybuf1, hot0, hot1, acc0, acc1, obuf, sem, osem):
    j = pl.program_id(0)
    n_tiles = pl.num_programs(0) - 1
    slot = lax.rem(j, 2)
    ybufs, hots, accs = (ybuf0, ybuf1), (hot0, hot1), (acc0, acc1)

    def fetch(s, tile):
        return [pltpu.make_async_copy(ye_hbm.at[e, _window(_expert_scalars(starts_ref, tile, e)[0], 0)],
                                      ybufs[s].at[e, pl.ds(0, CBW)], sem.at[s]) for e in range(N_EXPERTS)]

    @pl.when(j == 0)
    def _():
        for buf in (ybuf0, ybuf1, obuf):
            buf[...] = jnp.zeros_like(buf)
        for cp in fetch(0, 0):
            cp.start()
        hot0[...] = _scatter_onehot(post_ref[...])
        acc1[...] = jnp.zeros_like(acc1)

    def step(cur, nxt):
        @pl.when(j + 1 < n_tiles)
        def _():
            for cp in fetch(nxt, j + 1):
                cp.start()

        @pl.when(j < n_tiles)
        def _():
            for cp in fetch(cur, j):
                cp.wait()

        n_chunks = D_MODEL // MXU_COLS
        per_chunk = N_EXPERTS // n_chunks
        rows_chunk = TB // n_chunks
        col = lax.broadcasted_iota(jnp.int32, (1, SLOT), 1).astype(F32)
        for q in range(n_chunks):
            cols = slice(q * MXU_COLS, (q + 1) * MXU_COLS)
            accs[cur][:, cols] = _dot(hots[cur][...],
                                      ybufs[cur][:, :, cols].reshape(N_EXPERTS * SLOT, MXU_COLS))
            for e in range(q * per_chunk, (q + 1) * per_chunk):
                hots[nxt][:, e * SLOT:(e + 1) * SLOT] = _onehot(post_next_ref[:, e:e + 1] == col)
            rows = slice(q * rows_chunk, (q + 1) * rows_chunk)
            out_ref[rows, :] = _layer_norm(ALPHA * x1_ref[rows, :] + accs[nxt][rows, :],
                                           ln2g_ref[...], ln2b_ref[...])

        tile = jnp.minimum(j, n_tiles - 1)

        @pl.when(jnp.logical_and(j < n_tiles, _max_passes(_tile_scalars(starts_ref, tile)) > 1))
        def _():
            lane = lax.broadcasted_iota(jnp.int32, (1, LANES), 1)
            col = lax.broadcasted_iota(jnp.int32, (1, SLOT), 1)
            col = jnp.where(col < CBW, col, -SLOT).astype(F32)

            def expert(e, carry):
                a, off, n = _expert_scalars(starts_ref, tile, e)
                pcol = jnp.sum(jnp.where(lane == e, post_ref[...], 0.0), axis=1, keepdims=True)

                def extra(k, carry2):
                    cp = pltpu.make_async_copy(ye_hbm.at[e, _window(a, k)], obuf.at[pl.ds(0, CBW)], osem.at[0])
                    cp.start()
                    cp.wait()
                    accs[cur][...] += _dot(_onehot(pcol == col + (k * CBW).astype(F32)), obuf[...])
                    return carry2

                return lax.fori_loop(1, _passes(off, n), extra, carry)

            lax.fori_loop(0, N_EXPERTS, expert, 0)

    pl.when(slot == 0)(functools.partial(step, 0, 1))
    pl.when(slot == 1)(functools.partial(step, 1, 0))


def _combine(starts_tab, post, x1, ye, ln2_g, ln2_b):
    n_tok = post.shape[0]
    n_tiles = n_tok // TB
    assert n_tiles >= 2
    last = n_tiles - 1
    grid_spec = pltpu.PrefetchScalarGridSpec(
        num_scalar_prefetch=1,
        grid=(n_tiles + 1,),
        in_specs=[pl.BlockSpec((TB, LANES), lambda j, s: (jnp.minimum(j, last), 0)),
                  pl.BlockSpec((TB, LANES), lambda j, s: (jnp.minimum(j + 1, last), 0)),
                  pl.BlockSpec((TB, D_MODEL), lambda j, s: (jnp.maximum(j - 1, 0), 0)),
                  pl.BlockSpec((1, D_MODEL), lambda j, s: (0, 0)),
                  pl.BlockSpec((1, D_MODEL), lambda j, s: (0, 0)),
                  pl.BlockSpec(memory_space=pl.ANY)],
        out_specs=pl.BlockSpec((TB, D_MODEL), lambda j, s: (jnp.maximum(j - 1, 0), 0)),
        scratch_shapes=[pltpu.VMEM((N_EXPERTS, SLOT, D_MODEL), BF16),
                        pltpu.VMEM((N_EXPERTS, SLOT, D_MODEL), BF16),
                        pltpu.VMEM((TB, N_EXPERTS * SLOT), BF16),
                        pltpu.VMEM((TB, N_EXPERTS * SLOT), BF16),
                        pltpu.VMEM((TB, D_MODEL), F32),
                        pltpu.VMEM((TB, D_MODEL), F32),
                        pltpu.VMEM((SLOT, D_MODEL), BF16),
                        pltpu.SemaphoreType.DMA((2,)),
                        pltpu.SemaphoreType.DMA((1,))],
    )
    return pl.pallas_call(
        _combine_kernel,
        grid_spec=grid_spec,
        out_shape=jax.ShapeDtypeStruct((n_tok, D_MODEL), F32),
        compiler_params=pltpu.CompilerParams(dimension_semantics=("arbitrary",),
                                             vmem_limit_bytes=VMEM_LIMIT),
        name="combine",
    )(starts_tab, post, post, x1, ln2_g, ln2_b, ye)


def _trunk(x, p):
    b, s, d = x.shape
    n_tok = b * s
    cap = EC_CAPACITY_FACTOR * n_tok // N_EXPERTS
    n_tiles = n_tok // TB
    x1, xext, afft = _mixer(x.reshape(n_tok, d), s, p)
    pos, post, starts = _plan(afft, cap)
    starts_tab = jnp.concatenate([starts[:, :n_tiles].T, jnp.full((1, N_EXPERTS), cap, jnp.int32)], axis=0)
    xe = _dispatch(starts_tab, pos, xext, cap)
    ye = _ffn(xe, p["w_gate"], p["w_up"], p["w_down"], cap)
    y = _combine(starts_tab, post, x1, ye, p["ln2_g"], p["ln2_b"])
    return y.reshape(b, s, d)


def kernel(x_prompt, x_sample, ln_in_g, ln_in_b, w_in, pool_w, pool_scale, w_pool_proj, conv_w, w_conv_out, w_o, ln1_g, ln1_b, w_router, w_gate, w_up, w_down, ln2_g, ln2_b):
    assert w_in.shape == (1, D_MODEL, IN_COLS)
    wr = jnp.pad(w_router[0], ((0, 0), (0, LANES - N_EXPERTS)))
    wr_hi = wr.astype(BF16)
    wr_lo = (wr - wr_hi.astype(F32)).astype(BF16)
    p = {
        "ln_in_g": ln_in_g.reshape(1, D_MODEL), "ln_in_b": ln_in_b.reshape(1, D_MODEL),
        "w_in": w_in[0].astype(BF16), "pool_w": pool_w[0].astype(BF16), "pool_scale": pool_scale,
        "w_pool_proj": w_pool_proj[0].astype(BF16), "conv_w": conv_w[0],
        "w_conv_out": w_conv_out[0].astype(BF16), "w_o": w_o[0].astype(BF16),
        "ln1_g": ln1_g, "ln1_b": ln1_b,
        "wr_cat": jnp.concatenate([wr_hi, wr_lo], axis=1), "wr_hi": wr_hi,
        "w_gate": w_gate[0], "w_up": w_up[0], "w_down": w_down[0],
        "ln2_g": ln2_g, "ln2_b": ln2_b,
    }
    return _trunk(x_prompt, p), _trunk(x_sample, p)
```
